```python
import math
import jax
import jax.numpy as jnp
from jax import lax
import numpy as np

D_MODEL = 1024
BATCH = 8
SEQ = 2048
DEPTH = 4
DEC_BATCH = 128
DEC_SEQ = 1
PAST_LEN = 2048
PAGE_SIZE = 128

ATTN_GROUPS = ((128, 1), (512, 4), (2048, 16))
N_GROUPS = len(ATTN_GROUPS)
ATTN_HEADS = 4
ATTN_HEAD_DIM = 128
ATTN_OUT_WIDTH = ATTN_HEADS * ATTN_HEAD_DIM
ATTN_QKV_WIDTH = N_GROUPS * ATTN_OUT_WIDTH
ATTN_SCALE = ATTN_HEAD_DIM ** -0.5
HG_EXPAND = 128
HG_HEADS = D_MODEL // HG_EXPAND
HG_DK = HG_EXPAND
HG_DV = D_MODEL // HG_HEADS
HG_KWIDTH = HG_HEADS * HG_DK
HG_WIDTH = HG_HEADS * HG_DV
HG_CHUNK = 16
D_FF = 4 * D_MODEL
ROPE_THETA = 10000.0
NORM_EPS = 1e-6
IN_WIDTH = 3 * ATTN_QKV_WIDTH + 2 * HG_KWIDTH + 2 * HG_WIDTH + 2 * D_MODEL

kernel_name = 'dilated_attn_hgrn2_gated_hybrid_step'


def _split_points():
    sizes = (ATTN_QKV_WIDTH, ATTN_QKV_WIDTH, ATTN_QKV_WIDTH, HG_KWIDTH, HG_KWIDTH, HG_WIDTH, HG_WIDTH, D_MODEL, D_MODEL)
    pts, acc = [], 0
    for s in sizes[:-1]:
        acc += s
        pts.append(acc)
    return pts


def rms_norm(x, g):
    xf = x.astype(jnp.float32)
    y = xf * lax.rsqrt(jnp.mean(xf * xf, axis=-1, keepdims=True) + NORM_EPS)
    return (y * g.astype(jnp.float32)).astype(x.dtype)


def rope(x, pos):
    half = x.shape[-1] // 2
    inv = ROPE_THETA ** (-(jnp.arange(half, dtype=jnp.float32) * 2.0 / x.shape[-1]))
    ang = pos.astype(jnp.float32)[:, None] * inv[None, :]
    shp = (pos.shape[0],) + (1,) * (x.ndim - 3) + (half,)
    cos = jnp.cos(ang).reshape(shp)
    sin = jnp.sin(ang).reshape(shp)
    xf = x.astype(jnp.float32)
    x1, x2 = xf[..., :half], xf[..., half:]
    return jnp.concatenate([x1 * cos - x2 * sin, x2 * cos + x1 * sin], axis=-1)


def dilated_band_attn(q, k, v, dil, nk):
    B, S, H, Dh = q.shape
    L = S // dil
    nb = -(-L // nk)
    Lp = nb * nk
    pad_end = Lp - L

    def split(x):
        return x.reshape(B, L, dil, H, Dh).transpose(0, 2, 1, 3, 4)

    qb = jnp.pad(split(q), ((0, 0), (0, 0), (0, pad_end), (0, 0), (0, 0))).reshape(B, dil, nb, nk, H, Dh)

    def key_blocks(x):
        x = jnp.pad(split(x), ((0, 0), (0, 0), (nk, pad_end), (0, 0), (0, 0))).reshape(B, dil, nb + 1, nk, H, Dh)
        return jnp.concatenate([x[:, :, :-1], x[:, :, 1:]], axis=3)

    kb, vb = key_blocks(k), key_blocks(v)
    s = jnp.einsum('brnqhd,brnkhd->brnhqk', qb, kb) * ATTN_SCALE
    i = jnp.arange(nk)[:, None]
    j = jnp.arange(2 * nk)[None, :]
    steps = i + nk - j
    keypos = jnp.arange(nb)[:, None, None] * nk - nk + j[None]
    valid = (steps >= 0)[None] & (steps <= nk)[None] & (keypos >= 0)
    s = jnp.where(valid[:, None], s, -jnp.inf)
    m = jnp.max(s, axis=-1)
    p = jnp.exp(s - m[..., None])
    l = jnp.sum(p, axis=-1)
    o = jnp.einsum('brnhqk,brnkhd->brnqhd', p, vb) / jnp.swapaxes(l, 3, 4)[..., None]

    def back(t):
        t = t.reshape((B, dil, Lp) + t.shape[4:])[:, :, :L]
        t = jnp.swapaxes(t, 1, 2)
        return t.reshape((B, S) + t.shape[3:])

    return back(jnp.swapaxes(m, 3, 4)), back(jnp.swapaxes(l, 3, 4)), back(o)


def dilated_cached_attn(q, k_new, v_new, k_buf, v_buf, dil, nk):
    Wc = k_buf.shape[1]
    T = q.shape[1]
    k_ext = jnp.concatenate([k_buf, k_new], axis=1)
    v_ext = jnp.concatenate([v_buf, v_new], axis=1)
    idx = Wc + jnp.arange(T)[:, None] - dil * jnp.arange(nk + 1)[None, :]
    valid = idx >= 0
    idx_c = jnp.maximum(idx, 0)
    kg = jnp.take(k_ext, idx_c, axis=1)
    vg = jnp.take(v_ext, idx_c, axis=1)
    s = jnp.einsum('bthd,btkhd->bthk', q, kg) * ATTN_SCALE
    s = jnp.where(valid[:, None, :], s, -jnp.inf)
    m = jnp.max(s, axis=-1)
    p = jnp.exp(s - m[..., None])
    l = jnp.sum(p, axis=-1)
    o = jnp.einsum('bthk,btkhd->bthd', p, vg) / l[..., None]
    return m, l, o


def merge_groups(ms, ls, os_):
    m = jnp.stack(ms, 0)
    l = jnp.stack(ls, 0)
    o = jnp.stack(os_, 0)
    w = l * jnp.exp(m - jnp.max(m, axis=0, keepdims=True))
    return jnp.sum(w[..., None] * o, axis=0) / jnp.sum(w, axis=0)[..., None]


def hgrn2_chunked(q, log_f, k, v, s0, chunk):
    B, T, H, DK = q.shape
    n = T // chunk

    def to_chunks(x):
        return x.reshape(B, n, chunk, H, x.shape[-1]).transpose(1, 0, 3, 2, 4)

    mask = jnp.tril(jnp.ones((chunk, chunk), dtype=bool))

    def step(S, inp):
        qc, lfc, kc, vc = inp
        a = jnp.cumsum(lfc, axis=2)
        a_last = a[:, :, -1:, :]
        q_t = qc * jnp.exp(a)
        att = jnp.einsum('bhtd,bhsd->bhts', q_t, kc * jnp.exp(-a))
        att = jnp.where(mask, att, 0.0)
        o = jnp.einsum('bhts,bhse->bhte', att, vc) + jnp.einsum('bhtd,bhde->bhte', q_t, S)
        S = jnp.exp(a_last[:, :, 0, :])[..., None] * S + jnp.einsum('bhsd,bhse->bhde', kc * jnp.exp(a_last - a), vc)
        return S, o

    S, o = lax.scan(step, s0, (to_chunks(q), to_chunks(log_f), to_chunks(k), to_chunks(v)))
    o = o.transpose(1, 0, 3, 2, 4).reshape(B, T, H, v.shape[-1])
    return o, S


def block(x, pos, bufs, s0, lb, w_in, g1, gq, gk, g_hg, w_pa, w_pb, w_o, g2, w_up, w_down):
    B, T, _ = x.shape
    dt = x.dtype
    f32 = jnp.float32
    h = rms_norm(x, g1)
    proj = jnp.einsum('btd,de->bte', h, w_in)
    qa, ka, va, qh, fh, ih, gh, ga, gb = jnp.split(proj, _split_points(), axis=-1)

    def groups(t):
        return t.reshape(B, T, N_GROUPS, ATTN_HEADS, ATTN_HEAD_DIM)

    qa = rope(rms_norm(groups(qa), gq[:, None, :]), pos)
    ka = rope(rms_norm(groups(ka), gk[:, None, :]), pos)
    va = groups(va).astype(f32)
    ms, ls, os_, rows = [], [], [], []
    for gi, (win, dil) in enumerate(ATTN_GROUPS):
        nk = win // dil
        qg, kg, vg = qa[:, :, gi], ka[:, :, gi], va[:, :, gi]
        if bufs is None:
            m, l, o = dilated_band_attn(qg, kg, vg, dil, nk)
            keep = min(win, T)
            rows.append(kg[:, T - keep:].astype(dt))
            rows.append(vg[:, T - keep:].astype(dt))
        else:
            m, l, o = dilated_cached_attn(qg, kg, vg, bufs[2 * gi].astype(f32), bufs[2 * gi + 1].astype(f32), dil, nk)
            rows.append(kg.astype(dt))
            rows.append(vg.astype(dt))
        ms.append(m)
        ls.append(l)
        os_.append(o)
    ya = merge_groups(ms, ls, os_).reshape(B, T, ATTN_OUT_WIDTH).astype(dt)
    ya = jnp.einsum('btc,cd->btd', ya, w_pa)

    qh = jax.nn.silu(qh.astype(f32)).reshape(B, T, HG_HEADS, HG_DK)
    fg = lb + (1.0 - lb) * jax.nn.sigmoid(fh.astype(f32))
    log_f = jnp.log(fg).reshape(B, T, HG_HEADS, HG_DK)
    kh = (1.0 - fg).reshape(B, T, HG_HEADS, HG_DK)
    vh = ih.astype(f32).reshape(B, T, HG_HEADS, HG_DV)
    if s0 is None:
        s0 = jnp.zeros((B, HG_HEADS, HG_DK, HG_DV), f32)
    oh, s_new = hgrn2_chunked(qh, log_f, kh, vh, s0.astype(f32), math.gcd(T, HG_CHUNK))
    oh = rms_norm(oh, g_hg) * jax.nn.sigmoid(gh.astype(f32)).reshape(B, T, HG_HEADS, HG_DV)
    yb = jnp.einsum('btc,cd->btd', oh.reshape(B, T, HG_WIDTH).astype(dt), w_pb)

    merged = jax.nn.sigmoid(ga) * ya + jax.nn.sigmoid(gb) * yb
    x = x + jnp.einsum('btd,de->bte', merged, w_o)

    u = jnp.einsum('btd,df->btf', rms_norm(x, g2), w_up)
    x = x + jnp.einsum('btf,fd->btd', jnp.square(jax.nn.relu(u)), w_down)
    return x, rows, s_new.astype(dt)


def setup_inputs(seed: int = 0) -> dict:
    key = jax.random.key(seed)
    ks = jax.random.split(key, 24)
    f32 = jnp.float32

    def nrm(k, shape, scale):
        return jax.random.normal(k, shape, f32) * scale

    def gain(k, shape):
        return 1.0 + 0.02 * jax.random.normal(k, shape, f32)

    def kv(k, win):
        return nrm(k, (DEPTH, DEC_BATCH, min(win, PAST_LEN), ATTN_HEADS, ATTN_HEAD_DIM), 1.0)

    w0, w1, w2 = ATTN_GROUPS[0][0], ATTN_GROUPS[1][0], ATTN_GROUPS[2][0]
    return {
        'x_prompt': nrm(ks[0], (BATCH, SEQ, D_MODEL), 1.0),
        'x_sample': nrm(ks[1], (DEC_BATCH, DEC_SEQ, D_MODEL), 1.0),
        'cache_k_w128': kv(ks[2], w0),
        'cache_v_w128': kv(ks[3], w0),
        'cache_k_w512': kv(ks[4], w1),
        'cache_v_w512': kv(ks[5], w1),
        'cache_k_w2048': kv(ks[6], w2),
        'cache_v_w2048': kv(ks[7], w2),
        'state_hgrn': nrm(ks[8], (DEPTH, DEC_BATCH, HG_HEADS, HG_DK, HG_DV), 0.3),
        'w_in': nrm(ks[9], (DEPTH, D_MODEL, IN_WIDTH), D_MODEL ** -0.5),
        'norm1': gain(ks[10], (DEPTH, D_MODEL)),
        'q_norm': gain(ks[11], (DEPTH, N_GROUPS, ATTN_HEAD_DIM)),
        'k_norm': gain(ks[12], (DEPTH, N_GROUPS, ATTN_HEAD_DIM)),
        'hg_out_norm': gain(ks[13], (DEPTH, HG_DV)),
        'hg_lower_bounds': nrm(ks[14], (DEPTH, HG_KWIDTH), 0.1),
        'w_pa': nrm(ks[15], (DEPTH, ATTN_OUT_WIDTH, D_MODEL), ATTN_OUT_WIDTH ** -0.5),
        'w_pb': nrm(ks[16], (DEPTH, HG_WIDTH, D_MODEL), HG_WIDTH ** -0.5),
        'w_o': nrm(ks[17], (DEPTH, D_MODEL, D_MODEL), D_MODEL ** -0.5),
        'norm2': gain(ks[18], (DEPTH, D_MODEL)),
        'w_up': nrm(ks[19], (DEPTH, D_MODEL, D_FF), D_MODEL ** -0.5),
        'w_down': nrm(ks[20], (DEPTH, D_FF, D_MODEL), D_FF ** -0.5),
    }


def reference(x_prompt, x_sample, cache_k_w128, cache_v_w128, cache_k_w512, cache_v_w512, cache_k_w2048, cache_v_w2048, state_hgrn, w_in, norm1, q_norm, k_norm, hg_out_norm, hg_lower_bounds, w_pa, w_pb, w_o, norm2, w_up, w_down):
    lb_all = jnp.cumsum(jax.nn.softmax(hg_lower_bounds.astype(jnp.float32), axis=0), axis=0)
    lb_all = lb_all - lb_all[0:1]
    pos_p = jnp.arange(x_prompt.shape[1])
    pos_s = PAST_LEN + jnp.arange(x_sample.shape[1])
    caches = (cache_k_w128, cache_v_w128, cache_k_w512, cache_v_w512, cache_k_w2048, cache_v_w2048)
    yp, ys = x_prompt, x_sample
    p_rows = [[] for _ in range(2 * N_GROUPS)]
    s_rows = [[] for _ in range(2 * N_GROUPS)]
    p_states, s_states = [], []
    for layer in range(DEPTH):
        params = (w_in[layer], norm1[layer], q_norm[layer], k_norm[layer], hg_out_norm[layer],
                  w_pa[layer], w_pb[layer], w_o[layer], norm2[layer], w_up[layer], w_down[layer])
        yp, rows, st = block(yp, pos_p, None, None, lb_all[layer], *params)
        for c in range(2 * N_GROUPS):
            p_rows[c].append(rows[c])
        p_states.append(st)
        bufs = tuple(cache[layer] for cache in caches)
        ys, rows, st = block(ys, pos_s, bufs, state_hgrn[layer], lb_all[layer], *params)
        for c in range(2 * N_GROUPS):
            s_rows[c].append(rows[c])
        s_states.append(st)
    pk128, pv128, pk512, pv512, pk2048, pv2048 = [jnp.stack(r, axis=0) for r in p_rows]
    sk128, sv128, sk512, sv512, sk2048, sv2048 = [jnp.stack(r, axis=0) for r in s_rows]
    p_hgrn = jnp.stack(p_states, axis=0)
    s_hgrn = jnp.stack(s_states, axis=0)
    return (yp, ys, pk128, pv128, pk512, pv512, pk2048, pv2048, p_hgrn, sk128, sv128, sk512, sv512, sk2048, sv2048, s_hgrn)
```

```python
import functools

import jax
import jax.numpy as jnp
from jax import lax
from jax.experimental import pallas as pl
from jax.experimental.pallas import tpu as pltpu

F32 = jnp.float32
BF16 = jnp.bfloat16

D_MODEL = 1024
DEPTH = 4
PAST_LEN = 2048
ATTN_GROUPS = ((128, 1), (512, 4), (2048, 16))
N_GROUPS = 3
ATTN_HEADS = 4
HEAD_DIM = 128
GROUP_WIDTH = ATTN_HEADS * HEAD_DIM
QKV_WIDTH = N_GROUPS * GROUP_WIDTH
ATTN_SCALE = HEAD_DIM ** -0.5
HG_HEADS = 8
HG_DK = 128
HG_DV = 128
HG_WIDTH = 1024
HG_CHUNK = 16
D_FF = 4 * D_MODEL
ROPE_THETA = 10000.0
NORM_EPS = 1e-6
IN_WIDTH = 3 * QKV_WIDTH + 6 * HG_WIDTH
NEG_BIG = -1e30

COL_QH, COL_FH, COL_IH, COL_GH, COL_GA, COL_GB = (i * 1024 for i in range(6))
COL_QA = 6 * 1024
COL_KA = COL_QA + QKV_WIDTH
COL_VA = COL_KA + QKV_WIDTH
PROJ_TILE = 512
QK_TILE_LO = COL_QA // PROJ_TILE
QK_TILE_HI = COL_VA // PROJ_TILE

VMEM_LIMIT = 48 * 1024 * 1024


def _cparams(sem):
    return pltpu.CompilerParams(dimension_semantics=sem, vmem_limit_bytes=VMEM_LIMIT)


def _rms(x):
    return x * lax.rsqrt(jnp.mean(x * x, axis=-1, keepdims=True) + NORM_EPS)


def _in_proj_kernel(x_ref, g1_ref, w_ref, gqk_ref, cos_ref, sin_ref, o_ref, h_ref):
    j = pl.program_id(1)

    @pl.when(j == 0)
    def _():
        h_ref[...] = (_rms(x_ref[...]) * g1_ref[...]).astype(BF16)

    res = jnp.dot(h_ref[...], w_ref[...], preferred_element_type=F32)
    is_qk = jnp.logical_and(j >= QK_TILE_LO, j < QK_TILE_HI)

    @pl.when(is_qk)
    def _():
        cos = cos_ref[...]
        sin = sin_ref[...]
        for h in range(ATTN_HEADS):
            hs = slice(h * HEAD_DIM, (h + 1) * HEAD_DIM)
            y = _rms(res[:, hs]) * gqk_ref[:, hs]
            o_ref[:, hs] = y * cos + pltpu.roll(y, HEAD_DIM // 2, 1) * sin

    @pl.when(jnp.logical_not(is_qk))
    def _():
        o_ref[...] = res


def _in_proj(x, g1, w, gqk, cos, sin, tm):
    m = x.shape[0]
    n_pos_blocks = cos.shape[0] // tm
    n_tiles = IN_WIDTH // PROJ_TILE
    return pl.pallas_call(
        _in_proj_kernel,
        grid=(m // tm, n_tiles),
        in_specs=[
            pl.BlockSpec((tm, D_MODEL), lambda i, j: (i, 0)),
            pl.BlockSpec((1, D_MODEL), lambda i, j: (0, 0)),
            pl.BlockSpec((D_MODEL, PROJ_TILE), lambda i, j: (0, j)),
            pl.BlockSpec((1, PROJ_TILE),
                         lambda i, j: (0, jnp.clip(j - QK_TILE_LO, 0, QK_TILE_HI - QK_TILE_LO - 1))),
            pl.BlockSpec((tm, HEAD_DIM), lambda i, j: (i % n_pos_blocks, 0)),
            pl.BlockSpec((tm, HEAD_DIM), lambda i, j: (i % n_pos_blocks, 0)),
        ],
        out_specs=pl.BlockSpec((tm, PROJ_TILE), lambda i, j: (i, j)),
        out_shape=jax.ShapeDtypeStruct((m, IN_WIDTH), F32),
        scratch_shapes=[pltpu.VMEM((tm, D_MODEL), BF16)],
        compiler_params=_cparams(("parallel", "arbitrary")),
        name="in_proj",
    )(x, g1, w, gqk, cos, sin)


def _attn_prompt_kernel(q0, q1, q2, k0, k1, k2, v0, v1, v2, o_ref, m_sc, l_sc, a_sc):
    seq = o_ref.shape[0]
    nk = 128
    row = lax.broadcasted_iota(jnp.int32, (nk, nk), 0)
    col = lax.broadcasted_iota(jnp.int32, (nk, nk), 1)
    mask_cur = col <= row
    mask_prev = col >= row

    def rows(start, dil):
        if dil == 1:
            return pl.ds(start, nk)
        return pl.ds(start, nk, stride=dil)

    def scores(q, k_ref, sl, mask):
        k = k_ref[sl, :].astype(BF16)
        s = lax.dot_general(q, k, (((1,), (1,)), ((), ())), preferred_element_type=F32)
        return jnp.where(mask, s * ATTN_SCALE, NEG_BIG)

    def block(g, q_ref, k_ref, v_ref, start, prev_start, dil):
        sl = rows(start, dil)
        q = q_ref[sl, :].astype(BF16)
        s_cur = scores(q, k_ref, sl, mask_cur)
        m = jnp.max(s_cur, axis=-1, keepdims=True)
        if prev_start is not None:
            slp = rows(prev_start, dil)
            s_prev = scores(q, k_ref, slp, mask_prev)
            m = jnp.maximum(m, jnp.max(s_prev, axis=-1, keepdims=True))
        p_cur = jnp.exp(s_cur - m)
        l = jnp.sum(p_cur, axis=-1, keepdims=True)
        acc = jnp.dot(p_cur.astype(BF16), v_ref[sl, :].astype(BF16), preferred_element_type=F32)
        if prev_start is not None:
            p_prev = jnp.exp(s_prev - m)
            l = l + jnp.sum(p_prev, axis=-1, keepdims=True)
            acc = acc + jnp.dot(p_prev.astype(BF16), v_ref[slp, :].astype(BF16),
                                preferred_element_type=F32)
        m_sc[g, sl, :] = jnp.broadcast_to(m, (nk, HEAD_DIM))
        l_sc[g, sl, :] = jnp.broadcast_to(l, (nk, HEAD_DIM))
        a_sc[g, sl, :] = acc

    refs = ((q0, k0, v0), (q1, k1, v1), (q2, k2, v2))
    for g, (_, dil) in enumerate(ATTN_GROUPS):
        q_ref, k_ref, v_ref = refs[g]
        sub_len = seq // dil
        nb = sub_len // nk
        blk_stride = nk * dil
        if nb == 1:
            def body_r(r, carry, g=g, q_ref=q_ref, k_ref=k_ref, v_ref=v_ref, dil=dil):
                block(g, q_ref, k_ref, v_ref, r, None, dil)
                return carry
            lax.fori_loop(0, dil, body_r, 0)
        else:
            for r in range(dil):
                block(g, q_ref, k_ref, v_ref, r, None, dil)

                def body_n(n, carry, g=g, q_ref=q_ref, k_ref=k_ref, v_ref=v_ref, dil=dil,
                           r=r, blk_stride=blk_stride):
                    start = r + n * blk_stride
                    if dil == 1:
                        start = pl.multiple_of(start, nk)
                    block(g, q_ref, k_ref, v_ref, start, start - blk_stride, dil)
                    return carry
                lax.fori_loop(1, nb, body_n, 0)

    m_all = jnp.maximum(jnp.maximum(m_sc[0], m_sc[1]), m_sc[2])
    num = jnp.zeros((seq, HEAD_DIM), F32)
    den = jnp.zeros((seq, HEAD_DIM), F32)
    for g in range(N_GROUPS):
        w = jnp.exp(m_sc[g] - m_all)
        num = num + w * a_sc[g]
        den = den + w * l_sc[g]
    o_ref[...] = num / den


def _attn_prompt(proj3):
    b, seq, _ = proj3.shape

    def spec(col0, g):
        base = (col0 + g * GROUP_WIDTH) // HEAD_DIM
        return pl.BlockSpec((None, seq, HEAD_DIM), lambda bi, h, base=base: (bi, 0, base + h))

    in_specs = ([spec(COL_QA, g) for g in range(N_GROUPS)]
                + [spec(COL_KA, g) for g in range(N_GROUPS)]
                + [spec(COL_VA, g) for g in range(N_GROUPS)])
    return pl.pallas_call(
        _attn_prompt_kernel,
        grid=(b, ATTN_HEADS),
        in_specs=in_specs,
        out_specs=pl.BlockSpec((None, seq, HEAD_DIM), lambda bi, h: (bi, 0, h)),
        out_shape=jax.ShapeDtypeStruct((b, seq, GROUP_WIDTH), F32),
        scratch_shapes=[pltpu.VMEM((N_GROUPS, seq, HEAD_DIM), F32)] * 3,
        compiler_params=_cparams(("parallel", "parallel")),
        name="attn_prompt",
    )(*([proj3] * 9))


def _attn_sample_kernel(q_ref, k_ref, v_ref, kc0, vc0, kc1, vc1, kc2, vc2, o_ref):
    bb = q_ref.shape[0]
    caches = ((kc0, vc0), (kc1, vc1), (kc2, vc2))

    def body(b, carry):
        for h in range(ATTN_HEADS):
            hs = slice(h * HEAD_DIM, (h + 1) * HEAD_DIM)
            ms, ls, accs = [], [], []
            for g in range(N_GROUPS):
                cs = slice(g * GROUP_WIDTH + h * HEAD_DIM, g * GROUP_WIDTH + (h + 1) * HEAD_DIM)
                kc, vc = caches[g]
                q = q_ref[b, :, cs]
                kn = k_ref[b, :, cs]
                vn = v_ref[b, :, cs]
                kbuf = kc[b, :, hs].astype(BF16)
                vbuf = vc[b, :, hs].astype(BF16)
                q8 = jnp.broadcast_to(q, (8, HEAD_DIM)).astype(BF16)
                s = lax.dot_general(q8, kbuf, (((1,), (1,)), ((), ())),
                                    preferred_element_type=F32)[0:1] * ATTN_SCALE
                sn = jnp.sum(q * kn, axis=-1, keepdims=True) * ATTN_SCALE
                m = jnp.maximum(jnp.max(s, axis=-1, keepdims=True), sn)
                p = jnp.exp(s - m)
                pn = jnp.exp(sn - m)
                l = jnp.sum(p, axis=-1, keepdims=True) + pn
                p8 = jnp.broadcast_to(p, (8, HEAD_DIM)).astype(BF16)
                acc = jnp.dot(p8, vbuf, preferred_element_type=F32)[0:1] + pn * vn
                ms.append(m)
                ls.append(l)
                accs.append(acc)
            m_all = jnp.maximum(jnp.maximum(ms[0], ms[1]), ms[2])
            num = jnp.zeros((1, HEAD_DIM), F32)
            den = jnp.zeros((1, 1), F32)
            for g in range(N_GROUPS):
                w = jnp.exp(ms[g] - m_all)
                num = num + w * accs[g]
                den = den + w * ls[g]
            o_ref[b, :, hs] = num / den
        return carry

    lax.fori_loop(0, bb, body, 0)


def _attn_sample(proj, caches, layer, bb):
    m = proj.shape[0]
    qkv_specs = [pl.BlockSpec((bb, 1, QKV_WIDTH), lambda i, c=c: (i, 0, c))
                 for c in (COL_QA // QKV_WIDTH, COL_KA // QKV_WIDTH, COL_VA // QKV_WIDTH)]
    cache_specs = [pl.BlockSpec((None, bb, 128, GROUP_WIDTH), lambda i: (layer, i, 0, 0))
                   for _ in range(2 * N_GROUPS)]
    return pl.pallas_call(
        _attn_sample_kernel,
        grid=(m // bb,),
        in_specs=qkv_specs + cache_specs,
        out_specs=pl.BlockSpec((bb, 1, GROUP_WIDTH), lambda i: (i, 0, 0)),
        out_shape=jax.ShapeDtypeStruct((m, 1, GROUP_WIDTH), F32),
        compiler_params=_cparams(("parallel",)),
        name="attn_sample",
    )(proj, proj, proj, *caches)


def _cumsum_rows16(x):
    lo, hi = x[:8], x[8:]
    rid = lax.broadcasted_iota(jnp.int32, lo.shape, 0)

    def cs8(y):
        for s in (1, 2, 4):
            y = y + jnp.where(rid >= s, pltpu.roll(y, s, 0), 0.0)
        return y

    lo = cs8(lo)
    hi = cs8(hi) + lo[7:8, :]
    return jnp.concatenate([lo, hi], axis=0)


def _hgrn_gates(qh, fh, lb):
    fg = lb + (1.0 - lb) * jax.nn.sigmoid(fh)
    return jax.nn.silu(qh), fg


def _hgrn_prompt_kernel(qh_ref, fh_ref, ih_ref, gh_ref, lb_ref, gn_ref, o_ref, sfin_ref,
                        st_sc, oh_sc):
    t = pl.program_id(1)
    tc = qh_ref.shape[0]
    c = HG_CHUNK

    @pl.when(t == 0)
    def _():
        st_sc[...] = jnp.zeros_like(st_sc)

    lb = lb_ref[...]
    tril = (lax.broadcasted_iota(jnp.int32, (c, c), 1)
            <= lax.broadcasted_iota(jnp.int32, (c, c), 0))

    def chunk(ci, carry):
        r0 = pl.multiple_of(ci * c, c)
        rs = pl.ds(r0, c)
        q, fg = _hgrn_gates(qh_ref[rs, :], fh_ref[rs, :], lb)
        v = ih_ref[rs, :]
        a = _cumsum_rows16(jnp.log(fg))
        kh = 1.0 - fg
        a_last = a[c - 1:c, :]
        qt = (q * jnp.exp(a)).astype(BF16)
        kt = (kh * jnp.exp(-a)).astype(BF16)
        kd = (kh * jnp.exp(a_last - a)).astype(BF16)
        dec = jnp.exp(a_last)
        vb = v.astype(BF16)
        for h in range(HG_HEADS):
            hs = slice(h * HG_DK, (h + 1) * HG_DK)
            st = st_sc[h]
            att = lax.dot_general(qt[:, hs], kt[:, hs], (((1,), (1,)), ((), ())),
                                  preferred_element_type=F32)
            att = jnp.where(tril, att, 0.0).astype(BF16)
            o = jnp.dot(att, vb[:, hs], preferred_element_type=F32)
            o = o + lax.dot_general(qt[:, hs], st.astype(BF16), (((1,), (1,)), ((), ())),
                                    preferred_element_type=F32)
            oh_sc[rs, hs] = o
            kv_t = lax.dot_general(vb[:, hs], kd[:, hs], (((0,), (0,)), ((), ())),
                                   preferred_element_type=F32)
            st_sc[h] = st * dec[:, hs] + kv_t
        return carry

    lax.fori_loop(0, tc // c, chunk, 0)

    gate = jax.nn.sigmoid(gh_ref[...])
    for h in range(HG_HEADS):
        hs = slice(h * HG_DV, (h + 1) * HG_DV)
        o_ref[:, hs] = _rms(oh_sc[:, hs]) * gn_ref[...] * gate[:, hs]

    @pl.when(t == pl.num_programs(1) - 1)
    def _():
        for h in range(HG_HEADS):
            sfin_ref[h] = st_sc[h].T


def _hgrn_prompt(proj3, lb, gn, tc):
    b, seq, _ = proj3.shape

    def seg(col0):
        blk = col0 // HG_WIDTH
        return pl.BlockSpec((None, tc, HG_WIDTH), lambda bi, t, blk=blk: (bi, t, blk))

    return pl.pallas_call(
        _hgrn_prompt_kernel,
        grid=(b, seq // tc),
        in_specs=[seg(COL_QH), seg(COL_FH), seg(COL_IH), seg(COL_GH),
                  pl.BlockSpec((1, HG_WIDTH), lambda bi, t: (0, 0)),
                  pl.BlockSpec((1, HG_DV), lambda bi, t: (0, 0))],
        out_specs=[pl.BlockSpec((None, tc, HG_WIDTH), lambda bi, t: (bi, t, 0)),
                   pl.BlockSpec((None, HG_HEADS, HG_DK, HG_DV), lambda bi, t: (bi, 0, 0, 0))],
        out_shape=[jax.ShapeDtypeStruct((b, seq, HG_WIDTH), F32),
                   jax.ShapeDtypeStruct((b, HG_HEADS, HG_DK, HG_DV), F32)],
        scratch_shapes=[pltpu.VMEM((HG_HEADS, HG_DV, HG_DK), F32),
                        pltpu.VMEM((tc, HG_WIDTH), F32)],
        compiler_params=_cparams(("parallel", "arbitrary")),
        name="hgrn_prompt",
    )(proj3, proj3, proj3, proj3, lb, gn)


def _hgrn_sample_kernel(qh_ref, fh_ref, ih_ref, gh_ref, lb_ref, gn_ref, s_ref, o_ref, sout_ref):
    bb = qh_ref.shape[0]
    lb = lb_ref[...]
    ri = lax.broadcasted_iota(jnp.int32, (HG_DK, HG_DK), 0)
    ci = lax.broadcasted_iota(jnp.int32, (HG_DK, HG_DK), 1)
    diag = ri == ci

    def column(rowvec):
        full = jnp.broadcast_to(rowvec, (HG_DK, HG_DK))
        return jnp.sum(jnp.where(diag, full, 0.0), axis=1, keepdims=True)

    def body(b, carry):
        q, fg = _hgrn_gates(qh_ref[b], fh_ref[b], lb)
        v = ih_ref[b]
        gate = jax.nn.sigmoid(gh_ref[b])
        a = jnp.log(fg)
        kh = 1.0 - fg
        qt = q * jnp.exp(a)
        kt = kh * jnp.exp(-a)
        dec = jnp.exp(a)
        for h in range(HG_HEADS):
            hs = slice(h * HG_DK, (h + 1) * HG_DK)
            s0 = s_ref[b, h]
            att = jnp.sum(qt[:, hs] * kt[:, hs], axis=-1, keepdims=True)
            q8 = jnp.broadcast_to(qt[:, hs], (8, HG_DK)).astype(BF16)
            o = att * v[:, hs] + jnp.dot(q8, s0.astype(BF16), preferred_element_type=F32)[0:1]
            sout_ref[b, h] = column(dec[:, hs]) * s0 + column(kh[:, hs]) * v[:, hs]
            o_ref[b, :, hs] = _rms(o) * gn_ref[...] * gate[:, hs]
        return carry

    lax.fori_loop(0, bb, body, 0)


def _hgrn_sample(proj, lb, gn, state, layer, bb):
    m = proj.shape[0]

    def seg(col0):
        blk = col0 // HG_WIDTH
        return pl.BlockSpec((bb, 1, HG_WIDTH), lambda i, blk=blk: (i, 0, blk))

    return pl.pallas_call(
        _hgrn_sample_kernel,
        grid=(m // bb,),
        in_specs=[seg(COL_QH), seg(COL_FH), seg(COL_IH), seg(COL_GH),
                  pl.BlockSpec((1, HG_WIDTH), lambda i: (0, 0)),
                  pl.BlockSpec((1, HG_DV), lambda i: (0, 0)),
                  pl.BlockSpec((None, bb, HG_HEADS, HG_DK, HG_DV),
                               lambda i: (layer, i, 0, 0, 0))],
        out_specs=[pl.BlockSpec((bb, 1, HG_WIDTH), lambda i: (i, 0, 0)),
                   pl.BlockSpec((bb, HG_HEADS, HG_DK, HG_DV), lambda i: (i, 0, 0, 0))],
        out_shape=[jax.ShapeDtypeStruct((m, 1, HG_WIDTH), F32),
                   jax.ShapeDtypeStruct((m, HG_HEADS, HG_DK, HG_DV), F32)],
        compiler_params=_cparams(("parallel",)),
        name="hgrn_sample",
    )(proj, proj, proj, proj, lb, gn, state)


def _merge_kernel(x_ref, ya_ref, yb_ref, ga_ref, gb_ref, wpa_ref, wpb_ref, wo_ref, o_ref):
    pa = jnp.dot(ya_ref[...].astype(BF16), wpa_ref[...], preferred_element_type=F32)
    pb = jnp.dot(yb_ref[...].astype(BF16), wpb_ref[...], preferred_element_type=F32)
    merged = jax.nn.sigmoid(ga_ref[...]) * pa + jax.nn.sigmoid(gb_ref[...]) * pb
    o_ref[...] = x_ref[...] + jnp.dot(merged.astype(BF16), wo_ref[...],
                                      preferred_element_type=F32)


def _merge(x, ya, yb, proj, wpa, wpb, wo, tm):
    m = x.shape[0]
    row = lambda w: pl.BlockSpec((tm, w), lambda i: (i, 0))
    full = lambda a: pl.BlockSpec(a.shape, lambda i: (0, 0))
    return pl.pallas_call(
        _merge_kernel,
        grid=(m // tm,),
        in_specs=[row(D_MODEL), row(GROUP_WIDTH), row(HG_WIDTH),
                  pl.BlockSpec((tm, D_MODEL), lambda i: (i, COL_GA // D_MODEL)),
                  pl.BlockSpec((tm, D_MODEL), lambda i: (i, COL_GB // D_MODEL)),
                  full(wpa), full(wpb), full(wo)],
        out_specs=row(D_MODEL),
        out_shape=jax.ShapeDtypeStruct((m, D_MODEL), F32),
        compiler_params=_cparams(("parallel",)),
        name="merge",
    )(x, ya, yb, proj, proj, wpa, wpb, wo)


def _mlp_kernel(x_ref, g2_ref, wup_ref, wdn_ref, o_ref, h_ref):
    f = pl.program_id(1)

    @pl.when(f == 0)
    def _():
        x = x_ref[...]
        h_ref[...] = (_rms(x) * g2_ref[...]).astype(BF16)
        o_ref[...] = x

    u = jnp.dot(h_ref[...], wup_ref[...], preferred_element_type=F32)
    act = jnp.square(jnp.maximum(u, 0.0)).astype(BF16)
    o_ref[...] += jnp.dot(act, wdn_ref[...], preferred_element_type=F32)


def _mlp(x, g2, wup, wdn, tm, tf):
    m = x.shape[0]
    return pl.pallas_call(
        _mlp_kernel,
        grid=(m // tm, D_FF // tf),
        in_specs=[pl.BlockSpec((tm, D_MODEL), lambda i, f: (i, 0)),
                  pl.BlockSpec((1, D_MODEL), lambda i, f: (0, 0)),
                  pl.BlockSpec((D_MODEL, tf), lambda i, f: (0, f)),
                  pl.BlockSpec((tf, D_MODEL), lambda i, f: (f, 0))],
        out_specs=pl.BlockSpec((tm, D_MODEL), lambda i, f: (i, 0)),
        out_shape=jax.ShapeDtypeStruct((m, D_MODEL), F32),
        scratch_shapes=[pltpu.VMEM((tm, D_MODEL), BF16)],
        compiler_params=_cparams(("parallel", "arbitrary")),
        name="mlp",
    )(x, g2, wup, wdn)


def _rope_tables(pos):
    half = HEAD_DIM // 2
    inv = ROPE_THETA ** (-(jnp.arange(half, dtype=F32) * 2.0 / HEAD_DIM))
    ang = pos.astype(F32)[:, None] * inv[None, :]
    cos, sin = jnp.cos(ang), jnp.sin(ang)
    return jnp.concatenate([cos, cos], axis=-1), jnp.concatenate([-sin, sin], axis=-1)


def _reorder_w_in(w):
    return jnp.concatenate([w[:, 3 * QKV_WIDTH:], w[:, :3 * QKV_WIDTH]], axis=-1).astype(BF16)


def _heads(t, col0, g):
    lo = col0 + g * GROUP_WIDTH
    return t[..., lo:lo + GROUP_WIDTH]


def kernel(x_prompt, x_sample, cache_k_w128, cache_v_w128, cache_k_w512, cache_v_w512,
           cache_k_w2048, cache_v_w2048, state_hgrn, w_in, norm1, q_norm, k_norm, hg_out_norm,
           hg_lower_bounds, w_pa, w_pb, w_o, norm2, w_up, w_down):
    bsz, seq, _ = x_prompt.shape
    dec_b, dec_t, _ = x_sample.shape
    mp = bsz * seq
    ms = dec_b * dec_t

    lb_all = jnp.cumsum(jax.nn.softmax(hg_lower_bounds.astype(F32), axis=0), axis=0)
    lb_all = lb_all - lb_all[0:1]

    cos_p, sin_p = _rope_tables(jnp.arange(seq))
    cos_s, sin_s = _rope_tables(jnp.broadcast_to(PAST_LEN + jnp.arange(dec_t), (ms,)))

    caches = []
    for (win, dil), (ck, cv) in zip(ATTN_GROUPS, ((cache_k_w128, cache_v_w128),
                                                   (cache_k_w512, cache_v_w512),
                                                   (cache_k_w2048, cache_v_w2048))):
        wc = ck.shape[2]
        shape = (DEPTH, dec_b, wc // dil, dil * GROUP_WIDTH)
        caches += [ck.reshape(shape), cv.reshape(shape)]

    yp = x_prompt.reshape(mp, D_MODEL)
    ys = x_sample.reshape(ms, D_MODEL)
    p_rows = [[] for _ in range(2 * N_GROUPS)]
    s_rows = [[] for _ in range(2 * N_GROUPS)]
    p_states, s_states = [], []

    for layer in range(DEPTH):
        w_in_l = _reorder_w_in(w_in[layer])
        wpa, wpb, wo = (w[layer].astype(BF16) for w in (w_pa, w_pb, w_o))
        wup, wdn = w_up[layer].astype(BF16), w_down[layer].astype(BF16)
        g1 = norm1[layer].reshape(1, D_MODEL)
        g2 = norm2[layer].reshape(1, D_MODEL)
        gqk = jnp.concatenate([jnp.tile(q_norm[layer], (1, ATTN_HEADS)).reshape(1, QKV_WIDTH),
                               jnp.tile(k_norm[layer], (1, ATTN_HEADS)).reshape(1, QKV_WIDTH)],
                              axis=-1)
        gn = hg_out_norm[layer].reshape(1, HG_DV)
        lb = lb_all[layer].reshape(1, HG_WIDTH)

        proj = _in_proj(yp, g1, w_in_l, gqk, cos_p, sin_p, tm=1024)
        proj3 = proj.reshape(bsz, seq, IN_WIDTH)
        ya = _attn_prompt(proj3)
        yb, st = _hgrn_prompt(proj3, lb, gn, tc=256)
        yp = _merge(yp, ya.reshape(mp, GROUP_WIDTH), yb.reshape(mp, HG_WIDTH), proj,
                    wpa, wpb, wo, tm=512)
        yp = _mlp(yp, g2, wup, wdn, tm=512, tf=1024)
        for g, (win, _) in enumerate(ATTN_GROUPS):
            keep = min(win, seq)
            for c, col0 in enumerate((COL_KA, COL_VA)):
                p_rows[2 * g + c].append(
                    _heads(proj3, col0, g)[:, seq - keep:].reshape(bsz, keep, ATTN_HEADS, HEAD_DIM))
        p_states.append(st)

        proj_s = _in_proj(ys, g1, w_in_l, gqk, cos_s, sin_s, tm=ms)
        proj_s3 = proj_s.reshape(ms, 1, IN_WIDTH)
        ya_s = _attn_sample(proj_s3, caches, layer, bb=8)
        yb_s, st_s = _hgrn_sample(proj_s3, lb, gn, state_hgrn, layer, bb=8)
        ys = _merge(ys, ya_s.reshape(ms, GROUP_WIDTH), yb_s.reshape(ms, HG_WIDTH), proj_s,
                    wpa, wpb, wo, tm=ms)
        ys = _mlp(ys, g2, wup, wdn, tm=ms, tf=1024)
        for g in range(N_GROUPS):
            for c, col0 in enumerate((COL_KA, COL_VA)):
                s_rows[2 * g + c].append(
                    _heads(proj_s, col0, g).reshape(dec_b, dec_t, ATTN_HEADS, HEAD_DIM))
        s_states.append(st_s)

    outs = [yp.reshape(bsz, seq, D_MODEL), ys.reshape(dec_b, dec_t, D_MODEL)]
    outs += [jnp.stack(r, axis=0) for r in p_rows]
    outs.append(jnp.stack(p_states, axis=0))
    outs += [jnp.stack(r, axis=0) for r in s_rows]
    outs.append(jnp.stack(s_states, axis=0))
    return tuple(outs)
```

```python
import jax
import jax.numpy as jnp
from jax import lax
from jax.experimental import pallas as pl
from jax.experimental.pallas import tpu as pltpu

F32 = jnp.float32
BF16 = jnp.bfloat16

D_MODEL = 1024
DEPTH = 4
PAST_LEN = 2048
ATTN_GROUPS = ((128, 1), (512, 4), (2048, 16))
N_GROUPS = 3
ATTN_HEADS = 4
HEAD_DIM = 128
GROUP_WIDTH = ATTN_HEADS * HEAD_DIM
QKV_WIDTH = N_GROUPS * GROUP_WIDTH
ATTN_NK = 128
ATTN_SCALE = HEAD_DIM ** -0.5
HG_HEADS = 8
HG_DK = 128
HG_DV = 128
HG_WIDTH = 1024
HG_CHUNK = 16
HG_BLOCK = 128
HG_HEAD_UNROLL = 2
D_FF = 4 * D_MODEL
ROPE_THETA = 10000.0
NORM_EPS = 1e-6
IN_WIDTH = 3 * QKV_WIDTH + 6 * HG_WIDTH
GATE_WIDTH = 6 * HG_WIDTH
NEG_BIG = -1e30

COL_QH, COL_FH, COL_IH, COL_GH, COL_GA, COL_GB = (i * 1024 for i in range(6))
COL_Q, COL_K, COL_V = 0, QKV_WIDTH, 2 * QKV_WIDTH

GATE_TILE = 1536
QKV_TILE = GROUP_WIDTH
ROW_CHUNK = 256
ATTN_UNROLL = 4
SAMPLE_UNROLL = 2

VMEM_LIMIT = 48 * 1024 * 1024


def _cparams(sem):
    return pltpu.CompilerParams(dimension_semantics=sem, vmem_limit_bytes=VMEM_LIMIT)


def _rms(x):
    return x * lax.rsqrt(jnp.mean(x * x, axis=-1, keepdims=True) + NORM_EPS)


def _dot_nt(a, b):
    return lax.dot_general(a, b, (((1,), (1,)), ((), ())), preferred_element_type=F32)


def _in_proj_gate_kernel(x_ref, g1_ref, w_ref, o_ref, h_ref):
    @pl.when(pl.program_id(1) == 0)
    def _():
        h_ref[...] = (_rms(x_ref[...]) * g1_ref[...]).astype(BF16)

    tm = x_ref.shape[0]
    rc = min(tm, ROW_CHUNK)
    for c in range(tm // rc):
        rs = slice(c * rc, (c + 1) * rc)
        o_ref[rs, :] = jnp.dot(h_ref[rs, :], w_ref[...], preferred_element_type=F32)


def _in_proj_gate(x, g1, w, tm):
    m = x.shape[0]
    first = 3 * QKV_WIDTH // GATE_TILE
    return pl.pallas_call(
        _in_proj_gate_kernel,
        grid=(m // tm, GATE_WIDTH // GATE_TILE),
        in_specs=[
            pl.BlockSpec((tm, D_MODEL), lambda i, j: (i, 0)),
            pl.BlockSpec((1, D_MODEL), lambda i, j: (0, 0)),
            pl.BlockSpec((D_MODEL, GATE_TILE), lambda i, j: (0, first + j)),
        ],
        out_specs=pl.BlockSpec((tm, GATE_TILE), lambda i, j: (i, j)),
        out_shape=jax.ShapeDtypeStruct((m, GATE_WIDTH), F32),
        scratch_shapes=[pltpu.VMEM((tm, D_MODEL), BF16)],
        compiler_params=_cparams(("parallel", "arbitrary")),
        name="in_proj_gate",
    )(x, g1, w)


def _in_proj_qkv_kernel(x_ref, g1_ref, w_ref, gqk_ref, cos_ref, sin_ref, o_ref, rows_ref, h_ref):
    j = pl.program_id(1)

    @pl.when(j == 0)
    def _():
        h_ref[...] = (_rms(x_ref[...]) * g1_ref[...]).astype(BF16)

    tm = x_ref.shape[0]
    rc = min(tm, ROW_CHUNK)

    def tile(rope, rows):
        for c in range(tm // rc):
            rs = slice(c * rc, (c + 1) * rc)
            res = jnp.dot(h_ref[rs, :], w_ref[...], preferred_element_type=F32)
            for h in range(ATTN_HEADS):
                hs = slice(h * HEAD_DIM, (h + 1) * HEAD_DIM)
                y = res[:, hs]
                if rope:
                    y = _rms(y) * gqk_ref[:, hs]
                    y = y * cos_ref[rs, :] + pltpu.roll(y, HEAD_DIM // 2, 1) * sin_ref[rs, :]
                o_ref[rs, hs] = y
                if rows:
                    rows_ref[pl.ds(c * rc * ATTN_HEADS + h, rc, stride=ATTN_HEADS), :] = y

    pl.when(j < N_GROUPS)(lambda: tile(True, False))
    pl.when(jnp.logical_and(j >= N_GROUPS, j < 2 * N_GROUPS))(lambda: tile(True, True))
    pl.when(j >= 2 * N_GROUPS)(lambda: tile(False, True))


def _in_proj_qkv(x, g1, w, gqk, cos, sin, tm):
    m = x.shape[0]
    n_pos_blocks = cos.shape[0] // tm
    return pl.pallas_call(
        _in_proj_qkv_kernel,
        grid=(m // tm, 3 * N_GROUPS),
        in_specs=[
            pl.BlockSpec((tm, D_MODEL), lambda i, j: (i, 0)),
            pl.BlockSpec((1, D_MODEL), lambda i, j: (0, 0)),
            pl.BlockSpec((D_MODEL, QKV_TILE), lambda i, j: (0, j)),
            pl.BlockSpec((1, QKV_TILE), lambda i, j: (0, jnp.minimum(j, 2 * N_GROUPS - 1))),
            pl.BlockSpec((tm, HEAD_DIM), lambda i, j: (i % n_pos_blocks, 0)),
            pl.BlockSpec((tm, HEAD_DIM), lambda i, j: (i % n_pos_blocks, 0)),
        ],
        out_specs=[
            pl.BlockSpec((tm, QKV_TILE), lambda i, j: (i, j)),
            pl.BlockSpec((None, tm * ATTN_HEADS, HEAD_DIM),
                         lambda i, j: (jnp.maximum(j - N_GROUPS, 0), i, 0)),
        ],
        out_shape=[jax.ShapeDtypeStruct((m, 3 * QKV_WIDTH), F32),
                   jax.ShapeDtypeStruct((2 * N_GROUPS, m * ATTN_HEADS, HEAD_DIM), F32)],
        scratch_shapes=[pltpu.VMEM((tm, D_MODEL), BF16)],
        compiler_params=_cparams(("parallel", "arbitrary")),
        name="in_proj_qkv",
    )(x, g1, w, gqk, cos, sin)


def _attn_prompt_kernel(q0, q1, q2, k0, k1, k2, v0, v1, v2, o_ref, m_sc, l_sc, a_sc):
    seq = o_ref.shape[0]
    nk = ATTN_NK
    row = lax.broadcasted_iota(jnp.int32, (nk, nk), 0)
    col = lax.broadcasted_iota(jnp.int32, (nk, nk), 1)
    mask_cur = col <= row
    mask_prev = col >= row

    def rows(start, dil):
        if dil == 1:
            return pl.ds(start, nk)
        return pl.ds(start, nk, stride=dil)

    def blocks(g, qkv, dil, starts):
        q_ref, k_ref, v_ref = qkv
        loaded = []
        for start, prev_start in starts:
            sl = rows(start, dil)
            item = [sl, q_ref[sl, :], k_ref[sl, :], v_ref[sl, :]]
            if prev_start is not None:
                slp = rows(prev_start, dil)
                item += [k_ref[slp, :], v_ref[slp, :]]
            loaded.append(item)
        scored = []
        for item in loaded:
            q = item[1].astype(BF16)
            s = [jnp.where(mask_cur, _dot_nt(q, item[2].astype(BF16)) * ATTN_SCALE, NEG_BIG)]
            if len(item) > 4:
                s.append(jnp.where(mask_prev, _dot_nt(q, item[4].astype(BF16)) * ATTN_SCALE,
                                   NEG_BIG))
            scored.append(s)
        probs = []
        for s in scored:
            m = jnp.max(s[0] if len(s) == 1 else jnp.maximum(s[0], s[1]), axis=-1, keepdims=True)
            p = [jnp.exp(x - m) for x in s]
            l = jnp.sum(p[0] if len(s) == 1 else p[0] + p[1], axis=-1, keepdims=True)
            probs.append((m, l, [x.astype(BF16) for x in p]))
        for item, (m, l, p) in zip(loaded, probs):
            sl = item[0]
            acc = jnp.dot(p[0], item[3].astype(BF16), preferred_element_type=F32)
            if len(p) > 1:
                acc = acc + jnp.dot(p[1], item[5].astype(BF16), preferred_element_type=F32)
            m_sc[g, sl, :] = jnp.broadcast_to(m, (nk, HEAD_DIM))
            l_sc[g, sl, :] = jnp.broadcast_to(l, (nk, HEAD_DIM))
            a_sc[g, sl, :] = acc

    refs = ((q0, k0, v0), (q1, k1, v1), (q2, k2, v2))
    unr = ATTN_UNROLL
    for g, (_, dil) in enumerate(ATTN_GROUPS):
        qkv = refs[g]
        nb = seq // dil // nk
        stride = nk * dil
        if nb == 1:
            def body(it, carry, g=g, qkv=qkv, dil=dil):
                blocks(g, qkv, dil, [(it * unr + u, None) for u in range(unr)])
                return carry
            lax.fori_loop(0, dil // unr, body, 0)
        elif dil == 1:
            blocks(g, qkv, dil, [(n * nk, None if n == 0 else (n - 1) * nk) for n in range(unr)])

            def body(it, carry, g=g, qkv=qkv, dil=dil):
                starts = [pl.multiple_of((it * unr + u) * nk, nk) for u in range(unr)]
                blocks(g, qkv, dil, [(s, s - nk) for s in starts])
                return carry
            lax.fori_loop(1, nb // unr, body, 0)
        else:
            blocks(g, qkv, dil, [(r, None) for r in range(dil)])

            def body(n, carry, g=g, qkv=qkv, dil=dil, stride=stride):
                blocks(g, qkv, dil, [(r + n * stride, r + (n - 1) * stride) for r in range(dil)])
                return carry
            lax.fori_loop(1, nb, body, 0)

    m_all = jnp.maximum(jnp.maximum(m_sc[0], m_sc[1]), m_sc[2])
    num = jnp.zeros((seq, HEAD_DIM), F32)
    den = jnp.zeros((seq, HEAD_DIM), F32)
    for g in range(N_GROUPS):
        w = jnp.exp(m_sc[g] - m_all)
        num = num + w * a_sc[g]
        den = den + w * l_sc[g]
    o_ref[...] = num / den


def _attn_prompt(qkv3):
    b, seq, _ = qkv3.shape

    def spec(col0, g):
        base = (col0 + g * GROUP_WIDTH) // HEAD_DIM
        return pl.BlockSpec((None, seq, HEAD_DIM), lambda bi, h, base=base: (bi, 0, base + h))

    in_specs = ([spec(COL_Q, g) for g in range(N_GROUPS)]
                + [spec(COL_K, g) for g in range(N_GROUPS)]
                + [spec(COL_V, g) for g in range(N_GROUPS)])
    return pl.pallas_call(
        _attn_prompt_kernel,
        grid=(b, ATTN_HEADS),
        in_specs=in_specs,
        out_specs=pl.BlockSpec((None, seq, HEAD_DIM), lambda bi, h: (bi, 0, h)),
        out_shape=jax.ShapeDtypeStruct((b, seq, GROUP_WIDTH), F32),
        scratch_shapes=[pltpu.VMEM((N_GROUPS, seq, HEAD_DIM), F32)] * 3,
        compiler_params=_cparams(("parallel", "parallel")),
        name="attn_prompt",
    )(*([qkv3] * 9))


def _attn_sample_kernel(qkv_ref, kc0, vc0, kc1, vc1, kc2, vc2, o_ref):
    bb = o_ref.shape[0]
    caches = ((kc0, vc0), (kc1, vc1), (kc2, vc2))
    r8 = lax.broadcasted_iota(jnp.int32, (8, GROUP_WIDTH), 0)
    c8 = lax.broadcasted_iota(jnp.int32, (8, GROUP_WIDTH), 1)
    own = (r8 & (ATTN_HEADS - 1)) == (c8 >> 7)
    own_top = jnp.logical_and(own, r8 < ATTN_HEADS)

    def body(it, carry):
        items = []
        for u in range(SAMPLE_UNROLL):
            b = it * SAMPLE_UNROLL + u
            row = qkv_ref[b]
            for g in range(N_GROUPS):
                lo = g * GROUP_WIDTH
                q = row[:, COL_Q + lo:COL_Q + lo + GROUP_WIDTH]
                kn = row[:, COL_K + lo:COL_K + lo + GROUP_WIDTH]
                vn = row[:, COL_V + lo:COL_V + lo + GROUP_WIDTH]
                qbd = jnp.where(own, jnp.broadcast_to(q, (8, GROUP_WIDTH)), 0.0)
                s = _dot_nt(qbd.astype(BF16), caches[g][0][b].astype(BF16)) * ATTN_SCALE
                sn = jnp.sum(qbd * kn, axis=-1, keepdims=True) * ATTN_SCALE
                items.append((b, g, s, sn, vn))
        probs = []
        for b, g, s, sn, vn in items:
            m = jnp.maximum(jnp.max(s, axis=-1, keepdims=True), sn)
            p = jnp.exp(s - m)
            pn = jnp.exp(sn - m)
            probs.append((m, jnp.sum(p, axis=-1, keepdims=True) + pn, p.astype(BF16), pn))
        accs = [jnp.dot(p, caches[g][1][b].astype(BF16), preferred_element_type=F32) + pn * vn
                for (b, g, _, _, vn), (_, _, p, pn) in zip(items, probs)]
        for u in range(SAMPLE_UNROLL):
            sl = slice(u * N_GROUPS, (u + 1) * N_GROUPS)
            ms = [pr[0] for pr in probs[sl]]
            ls = [pr[1] for pr in probs[sl]]
            m_all = jnp.maximum(jnp.maximum(ms[0], ms[1]), ms[2])
            num = jnp.zeros((8, GROUP_WIDTH), F32)
            den = jnp.zeros((8, 1), F32)
            for g in range(N_GROUPS):
                w = jnp.exp(ms[g] - m_all)
                num = num + w * accs[sl][g]
                den = den + w * ls[g]
            o_ref[it * SAMPLE_UNROLL + u] = jnp.sum(jnp.where(own_top, num / den, 0.0),
                                                    axis=0, keepdims=True)
        return carry

    lax.fori_loop(0, bb // SAMPLE_UNROLL, body, 0)


def _attn_sample(qkv3, caches, layer, bb):
    m = qkv3.shape[0]
    cache_specs = [pl.BlockSpec((None, bb, ATTN_NK, GROUP_WIDTH), lambda i: (layer, i, 0, 0))
                   for _ in range(2 * N_GROUPS)]
    return pl.pallas_call(
        _attn_sample_kernel,
        grid=(m // bb,),
        in_specs=[pl.BlockSpec((bb, 1, 3 * QKV_WIDTH), lambda i: (i, 0, 0))] + cache_specs,
        out_specs=pl.BlockSpec((bb, 1, GROUP_WIDTH), lambda i: (i, 0, 0)),
        out_shape=jax.ShapeDtypeStruct((m, 1, GROUP_WIDTH), F32),
        compiler_params=_cparams(("parallel",)),
        name="attn_sample",
    )(qkv3, *caches)


def _hgrn_gates(qh, fh, lb):
    fg = lb + (1.0 - lb) * jax.nn.sigmoid(fh)
    return jax.nn.silu(qh), fg


def _cumsum8(y, rid):
    for s in (1, 2, 4):
        y = y + jnp.where(rid >= s, pltpu.roll(y, s, 0), 0.0)
    return y


def _hgrn_prompt_kernel(qh_ref, fh_ref, ih_ref, gh_ref, lb_ref, gn_ref, o_ref, sfin_ref, st_sc):
    t = pl.program_id(1)
    blk = HG_BLOCK
    c = HG_CHUNK
    nch = blk // c

    @pl.when(t == 0)
    def _():
        st_sc[...] = jnp.zeros_like(st_sc)

    rowi = lax.broadcasted_iota(jnp.int32, (blk, blk), 0)
    coli = lax.broadcasted_iota(jnp.int32, (blk, blk), 1)
    causal = coli <= rowi
    rid8 = lax.broadcasted_iota(jnp.int32, (8, HG_DK), 0)

    def head_scores(qh, fh, lb):
        q, fg = _hgrn_gates(qh, fh, lb)
        lf = jnp.log(fg)
        kh = 1.0 - fg

        slabs = []
        for k in range(blk // 8):
            s8 = _cumsum8(lf[8 * k:8 * k + 8], rid8)
            if k % 2 == 1:
                s8 = s8 + slabs[k - 1][7:8, :]
            slabs.append(s8)
        g_pre = jnp.zeros((1, HG_DK), F32)
        g_starts, acum_slabs = [], []
        for i in range(nch):
            g_starts.append(g_pre)
            acum_slabs += [slabs[2 * i] + g_pre, slabs[2 * i + 1] + g_pre]
            g_pre = g_pre + slabs[2 * i + 1][7:8, :]
        g_end = g_pre
        a = jnp.concatenate(slabs, axis=0)
        acum = jnp.concatenate(acum_slabs, axis=0)

        qt = (q * jnp.exp(a)).astype(BF16)
        qhat = (q * jnp.exp(acum)).astype(BF16)
        khat = (kh * jnp.exp(g_end - acum)).astype(BF16)

        prow = []
        for i in range(nch):
            n = c * (i + 1)
            kt = (kh[:n] * jnp.exp(g_starts[i] - acum[:n])).astype(BF16)
            if n < blk:
                kt = jnp.concatenate([kt, jnp.zeros((blk - n, HG_DK), BF16)], axis=0)
            prow.append(_dot_nt(qt[c * i:c * (i + 1)], kt))
        return prow, qhat, khat, jnp.exp(g_end)

    def head_output(scores, v, gh, st0):
        prow, qhat, khat, dec = scores
        p = jnp.where(causal, jnp.concatenate(prow, axis=0), 0.0).astype(BF16)
        vt = v.T.astype(BF16)
        o = _dot_nt(jnp.concatenate([p, qhat], axis=1),
                    jnp.concatenate([vt, st0.astype(BF16)], axis=1))
        o = _rms(o) * gn_ref[...] * jax.nn.sigmoid(gh)
        return o, st0 * dec + jnp.dot(vt, khat, preferred_element_type=F32)

    def head_group(i, carry):
        heads = [i * HG_HEAD_UNROLL + u for u in range(HG_HEAD_UNROLL)]
        cols = [pl.ds(pl.multiple_of(h * HG_DK, HG_DK), HG_DK) for h in heads]
        loaded = [(qh_ref[:, cs], fh_ref[:, cs], lb_ref[:, cs], ih_ref[:, cs], gh_ref[:, cs],
                   st_sc[h]) for h, cs in zip(heads, cols)]
        scores = [head_scores(*args[:3]) for args in loaded]
        results = [head_output(sc, *args[3:]) for sc, args in zip(scores, loaded)]
        for h, cs, (o, st_new) in zip(heads, cols, results):
            o_ref[:, cs] = o
            st_sc[h] = st_new
        return carry

    lax.fori_loop(0, HG_HEADS // HG_HEAD_UNROLL, head_group, 0)

    @pl.when(t == pl.num_programs(1) - 1)
    def _():
        for h in range(HG_HEADS):
            sfin_ref[h] = st_sc[h].T


def _hgrn_prompt(gate3, lb, gn):
    b, seq, _ = gate3.shape

    def seg(col0):
        blk = col0 // HG_WIDTH
        return pl.BlockSpec((None, HG_BLOCK, HG_WIDTH), lambda bi, t, blk=blk: (bi, t, blk))

    return pl.pallas_call(
        _hgrn_prompt_kernel,
        grid=(b, seq // HG_BLOCK),
        in_specs=[seg(COL_QH), seg(COL_FH), seg(COL_IH), seg(COL_GH),
                  pl.BlockSpec((1, HG_WIDTH), lambda bi, t: (0, 0)),
                  pl.BlockSpec((1, HG_DV), lambda bi, t: (0, 0))],
        out_specs=[pl.BlockSpec((None, HG_BLOCK, HG_WIDTH), lambda bi, t: (bi, t, 0)),
                   pl.BlockSpec((None, HG_HEADS, HG_DK, HG_DV), lambda bi, t: (bi, 0, 0, 0))],
        out_shape=[jax.ShapeDtypeStruct((b, seq, HG_WIDTH), F32),
                   jax.ShapeDtypeStruct((b, HG_HEADS, HG_DK, HG_DV), F32)],
        scratch_shapes=[pltpu.VMEM((HG_HEADS, HG_DV, HG_DK), F32)],
        compiler_params=_cparams(("parallel", "arbitrary")),
        name="hgrn_prompt",
    )(gate3, gate3, gate3, gate3, lb, gn)


def _hgrn_sample_kernel(qh_ref, fh_ref, ih_ref, gh_ref, lb_ref, gn_ref, s_ref, o_ref, sout_ref):
    bb = qh_ref.shape[0]
    lb = lb_ref[...]
    ri = lax.broadcasted_iota(jnp.int32, (HG_DK, HG_DK), 0)
    ci = lax.broadcasted_iota(jnp.int32, (HG_DK, HG_DK), 1)
    diag = ri == ci

    def column(rowvec):
        full = jnp.broadcast_to(rowvec, (HG_DK, HG_DK))
        return jnp.sum(jnp.where(diag, full, 0.0), axis=1, keepdims=True)

    def body(b, carry):
        q, fg = _hgrn_gates(qh_ref[b], fh_ref[b], lb)
        v = ih_ref[b]
        gate = jax.nn.sigmoid(gh_ref[b])
        a = jnp.log(fg)
        kh = 1.0 - fg
        qt = q * jnp.exp(a)
        kt = kh * jnp.exp(-a)
        dec = jnp.exp(a)
        for h in range(HG_HEADS):
            hs = slice(h * HG_DK, (h + 1) * HG_DK)
            s0 = s_ref[b, h]
            att = jnp.sum(qt[:, hs] * kt[:, hs], axis=-1, keepdims=True)
            q8 = jnp.broadcast_to(qt[:, hs], (8, HG_DK)).astype(BF16)
            o = att * v[:, hs] + jnp.dot(q8, s0.astype(BF16), preferred_element_type=F32)[0:1]
            sout_ref[b, h] = column(dec[:, hs]) * s0 + column(kh[:, hs]) * v[:, hs]
            o_ref[b, :, hs] = _rms(o) * gn_ref[...] * gate[:, hs]
        return carry

    lax.fori_loop(0, bb, body, 0)


def _hgrn_sample(gate3, lb, gn, state, layer, bb):
    m = gate3.shape[0]

    def seg(col0):
        blk = col0 // HG_WIDTH
        return pl.BlockSpec((bb, 1, HG_WIDTH), lambda i, blk=blk: (i, 0, blk))

    return pl.pallas_call(
        _hgrn_sample_kernel,
        grid=(m // bb,),
        in_specs=[seg(COL_QH), seg(COL_FH), seg(COL_IH), seg(COL_GH),
                  pl.BlockSpec((1, HG_WIDTH), lambda i: (0, 0)),
                  pl.BlockSpec((1, HG_DV), lambda i: (0, 0)),
                  pl.BlockSpec((None, bb, HG_HEADS, HG_DK, HG_DV),
                               lambda i: (layer, i, 0, 0, 0))],
        out_specs=[pl.BlockSpec((bb, 1, HG_WIDTH), lambda i: (i, 0, 0)),
                   pl.BlockSpec((bb, HG_HEADS, HG_DK, HG_DV), lambda i: (i, 0, 0, 0))],
        out_shape=[jax.ShapeDtypeStruct((m, 1, HG_WIDTH), F32),
                   jax.ShapeDtypeStruct((m, HG_HEADS, HG_DK, HG_DV), F32)],
        compiler_params=_cparams(("parallel",)),
        name="hgrn_sample",
    )(gate3, gate3, gate3, gate3, lb, gn, state)


def _merge_kernel(x_ref, ya_ref, yb_ref, ga_ref, gb_ref, wpa_ref, wpb_ref, wo_ref, o_ref):
    tm = x_ref.shape[0]
    rc = min(tm, ROW_CHUNK)

    def branches(c):
        rs = slice(c * rc, (c + 1) * rc)
        return (jnp.dot(ya_ref[rs, :].astype(BF16), wpa_ref[...], preferred_element_type=F32),
                jnp.dot(yb_ref[rs, :].astype(BF16), wpb_ref[...], preferred_element_type=F32))

    nxt = branches(0)
    for c in range(tm // rc):
        rs = slice(c * rc, (c + 1) * rc)
        pa, pb = nxt
        if c + 1 < tm // rc:
            nxt = branches(c + 1)
        merged = jax.nn.sigmoid(ga_ref[rs, :]) * pa + jax.nn.sigmoid(gb_ref[rs, :]) * pb
        o_ref[rs, :] = x_ref[rs, :] + jnp.dot(merged.astype(BF16), wo_ref[...],
                                              preferred_element_type=F32)


def _merge(x, ya, yb, gate, wpa, wpb, wo, tm):
    m = x.shape[0]
    row = lambda w: pl.BlockSpec((tm, w), lambda i: (i, 0))
    full = lambda a: pl.BlockSpec(a.shape, lambda i: (0, 0))
    return pl.pallas_call(
        _merge_kernel,
        grid=(m // tm,),
        in_specs=[row(D_MODEL), row(GROUP_WIDTH), row(HG_WIDTH),
                  pl.BlockSpec((tm, D_MODEL), lambda i: (i, COL_GA // D_MODEL)),
                  pl.BlockSpec((tm, D_MODEL), lambda i: (i, COL_GB // D_MODEL)),
                  full(wpa), full(wpb), full(wo)],
        out_specs=row(D_MODEL),
        out_shape=jax.ShapeDtypeStruct((m, D_MODEL), F32),
        compiler_params=_cparams(("parallel",)),
        name="merge",
    )(x, ya, yb, gate, gate, wpa, wpb, wo)


def _mlp_kernel(x_ref, g2_ref, wup_ref, wdn_ref, o_ref, h_ref):
    f = pl.program_id(1)

    @pl.when(f == 0)
    def _():
        x = x_ref[...]
        h_ref[...] = (_rms(x) * g2_ref[...]).astype(BF16)
        o_ref[...] = x

    tm = x_ref.shape[0]
    rc = min(tm, ROW_CHUNK)

    def up(c):
        return jnp.dot(h_ref[c * rc:(c + 1) * rc, :], wup_ref[...], preferred_element_type=F32)

    nxt = up(0)
    for c in range(tm // rc):
        rs = slice(c * rc, (c + 1) * rc)
        u = nxt
        if c + 1 < tm // rc:
            nxt = up(c + 1)
        act = jnp.square(jnp.maximum(u, 0.0)).astype(BF16)
        o_ref[rs, :] += jnp.dot(act, wdn_ref[...], preferred_element_type=F32)


def _mlp(x, g2, wup, wdn, tm, tf):
    m = x.shape[0]
    return pl.pallas_call(
        _mlp_kernel,
        grid=(m // tm, D_FF // tf),
        in_specs=[pl.BlockSpec((tm, D_MODEL), lambda i, f: (i, 0)),
                  pl.BlockSpec((1, D_MODEL), lambda i, f: (0, 0)),
                  pl.BlockSpec((D_MODEL, tf), lambda i, f: (0, f)),
                  pl.BlockSpec((tf, D_MODEL), lambda i, f: (f, 0))],
        out_specs=pl.BlockSpec((tm, D_MODEL), lambda i, f: (i, 0)),
        out_shape=jax.ShapeDtypeStruct((m, D_MODEL), F32),
        scratch_shapes=[pltpu.VMEM((tm, D_MODEL), BF16)],
        compiler_params=_cparams(("parallel", "arbitrary")),
        name="mlp",
    )(x, g2, wup, wdn)


def _rope_tables(pos):
    half = HEAD_DIM // 2
    inv = ROPE_THETA ** (-(jnp.arange(half, dtype=F32) * 2.0 / HEAD_DIM))
    ang = pos.astype(F32)[:, None] * inv[None, :]
    cos, sin = jnp.cos(ang), jnp.sin(ang)
    return jnp.concatenate([cos, cos], axis=-1), jnp.concatenate([-sin, sin], axis=-1)


def kernel(x_prompt, x_sample, cache_k_w128, cache_v_w128, cache_k_w512, cache_v_w512,
           cache_k_w2048, cache_v_w2048, state_hgrn, w_in, norm1, q_norm, k_norm, hg_out_norm,
           hg_lower_bounds, w_pa, w_pb, w_o, norm2, w_up, w_down):
    bsz, seq, _ = x_prompt.shape
    dec_b, dec_t, _ = x_sample.shape
    mp = bsz * seq
    ms = dec_b * dec_t

    lb_all = jnp.cumsum(jax.nn.softmax(hg_lower_bounds.astype(F32), axis=0), axis=0)
    lb_all = lb_all - lb_all[0:1]

    cos_p, sin_p = _rope_tables(jnp.arange(seq))
    cos_s, sin_s = _rope_tables(jnp.broadcast_to(PAST_LEN + jnp.arange(dec_t), (ms,)))

    caches = []
    for (win, dil), (ck, cv) in zip(ATTN_GROUPS, ((cache_k_w128, cache_v_w128),
                                                   (cache_k_w512, cache_v_w512),
                                                   (cache_k_w2048, cache_v_w2048))):
        wc = ck.shape[2]
        shape = (DEPTH, dec_b, wc // dil, dil * GROUP_WIDTH)
        caches += [ck.reshape(shape), cv.reshape(shape)]

    w_in_b, w_pa_b, w_pb_b, w_o_b, w_up_b, w_dn_b = (
        w.astype(BF16) for w in (w_in, w_pa, w_pb, w_o, w_up, w_down))

    yp = x_prompt.reshape(mp, D_MODEL)
    ys = x_sample.reshape(ms, D_MODEL)
    p_rows = [[] for _ in range(2 * N_GROUPS)]
    s_rows = [[] for _ in range(2 * N_GROUPS)]
    p_states, s_states = [], []

    for layer in range(DEPTH):
        w_in_l = w_in_b[layer]
        wpa, wpb, wo = w_pa_b[layer], w_pb_b[layer], w_o_b[layer]
        wup, wdn = w_up_b[layer], w_dn_b[layer]
        g1 = norm1[layer].reshape(1, D_MODEL)
        g2 = norm2[layer].reshape(1, D_MODEL)
        gqk = jnp.concatenate([jnp.tile(q_norm[layer], (1, ATTN_HEADS)).reshape(1, QKV_WIDTH),
                               jnp.tile(k_norm[layer], (1, ATTN_HEADS)).reshape(1, QKV_WIDTH)],
                              axis=-1)
        gn = hg_out_norm[layer].reshape(1, HG_DV)
        lb = lb_all[layer].reshape(1, HG_WIDTH)

        gate = _in_proj_gate(yp, g1, w_in_l, tm=1024)
        qkv, rows = _in_proj_qkv(yp, g1, w_in_l, gqk, cos_p, sin_p, tm=1024)
        ya = _attn_prompt(qkv.reshape(bsz, seq, 3 * QKV_WIDTH))
        yb, st = _hgrn_prompt(gate.reshape(bsz, seq, GATE_WIDTH), lb, gn)
        yp = _merge(yp, ya.reshape(mp, GROUP_WIDTH), yb.reshape(mp, HG_WIDTH), gate,
                    wpa, wpb, wo, tm=512)
        yp = _mlp(yp, g2, wup, wdn, tm=512, tf=1024)
        rows5 = rows.reshape(2 * N_GROUPS, bsz, seq, ATTN_HEADS, HEAD_DIM)
        for g, (win, _) in enumerate(ATTN_GROUPS):
            keep = min(win, seq)
            for c in range(2):
                p_rows[2 * g + c].append(rows5[c * N_GROUPS + g, :, seq - keep:])
        p_states.append(st)

        gate_s = _in_proj_gate(ys, g1, w_in_l, tm=ms)
        qkv_s, rows_s = _in_proj_qkv(ys, g1, w_in_l, gqk, cos_s, sin_s, tm=ms)
        ya_s = _attn_sample(qkv_s.reshape(ms, 1, 3 * QKV_WIDTH), caches, layer, bb=8)
        yb_s, st_s = _hgrn_sample(gate_s.reshape(ms, 1, GATE_WIDTH), lb, gn, state_hgrn, layer, bb=8)
        ys = _merge(ys, ya_s.reshape(ms, GROUP_WIDTH), yb_s.reshape(ms, HG_WIDTH), gate_s,
                    wpa, wpb, wo, tm=ms)
        ys = _mlp(ys, g2, wup, wdn, tm=ms, tf=1024)
        rows5_s = rows_s.reshape(2 * N_GROUPS, dec_b, dec_t, ATTN_HEADS, HEAD_DIM)
        for g in range(N_GROUPS):
            for c in range(2):
                s_rows[2 * g + c].append(rows5_s[c * N_GROUPS + g])
        s_states.append(st_s)

    outs = [yp.reshape(bsz, seq, D_MODEL), ys.reshape(dec_b, dec_t, D_MODEL)]
    outs += [jnp.stack(r, axis=0) for r in p_rows]
    outs.append(jnp.stack(p_states, axis=0))
    outs += [jnp.stack(r, axis=0) for r in s_rows]
    outs.append(jnp.stack(s_states, axis=0))
    return tuple(outs)
```

```python
import jax
import jax.numpy as jnp
from jax import lax
from jax.experimental import pallas as pl
from jax.experimental.pallas import tpu as pltpu

F32 = jnp.float32
BF16 = jnp.bfloat16

D_MODEL = 1024
DEPTH = 4
PAST_LEN = 2048
ATTN_GROUPS = ((128, 1), (512, 4), (2048, 16))
N_GROUPS = 3
ATTN_HEADS = 4
HEAD_DIM = 128
GROUP_WIDTH = ATTN_HEADS * HEAD_DIM
QKV_WIDTH = N_GROUPS * GROUP_WIDTH
ATTN_NK = 128
ATTN_SCALE = HEAD_DIM ** -0.5
HG_HEADS = 8
HG_DK = 128
HG_DV = 128
HG_WIDTH = 1024
HG_CHUNK = 16
HG_BLOCK = 128
HG_HEAD_UNROLL = 4
D_FF = 4 * D_MODEL
ROPE_THETA = 10000.0
NORM_EPS = 1e-6
IN_WIDTH = 3 * QKV_WIDTH + 6 * HG_WIDTH
GATE_WIDTH = 6 * HG_WIDTH
NEG_BIG = -1e30

COL_QH, COL_FH, COL_IH, COL_GH, COL_GA, COL_GB = (i * 1024 for i in range(6))
COL_Q, COL_K, COL_V = 0, QKV_WIDTH, 2 * QKV_WIDTH

GATE_TILE = 1536
QKV_TILE = GROUP_WIDTH
ROW_CHUNK = 256
ATTN_UNROLL = 4
SAMPLE_UNROLL = 2

VMEM_LIMIT = 48 * 1024 * 1024


def _cparams(sem):
    return pltpu.CompilerParams(dimension_semantics=sem, vmem_limit_bytes=VMEM_LIMIT)


def _rms(x):
    return x * lax.rsqrt(jnp.mean(x * x, axis=-1, keepdims=True) + NORM_EPS)


def _dot_nt(a, b):
    return lax.dot_general(a, b, (((1,), (1,)), ((), ())), preferred_element_type=F32)


def _in_proj_gate_kernel(x_ref, g1_ref, w_ref, o_ref, h_ref):
    @pl.when(pl.program_id(1) == 0)
    def _():
        h_ref[...] = (_rms(x_ref[...]) * g1_ref[...]).astype(BF16)

    tm = x_ref.shape[0]
    rc = min(tm, ROW_CHUNK)
    for c in range(tm // rc):
        rs = slice(c * rc, (c + 1) * rc)
        o_ref[rs, :] = jnp.dot(h_ref[rs, :], w_ref[...], preferred_element_type=F32)


def _in_proj_gate(x, g1, w, tm):
    m = x.shape[0]
    first = 3 * QKV_WIDTH // GATE_TILE
    return pl.pallas_call(
        _in_proj_gate_kernel,
        grid=(m // tm, GATE_WIDTH // GATE_TILE),
        in_specs=[
            pl.BlockSpec((tm, D_MODEL), lambda i, j: (i, 0)),
            pl.BlockSpec((1, D_MODEL), lambda i, j: (0, 0)),
            pl.BlockSpec((D_MODEL, GATE_TILE), lambda i, j: (0, first + j)),
        ],
        out_specs=pl.BlockSpec((tm, GATE_TILE), lambda i, j: (i, j)),
        out_shape=jax.ShapeDtypeStruct((m, GATE_WIDTH), F32),
        scratch_shapes=[pltpu.VMEM((tm, D_MODEL), BF16)],
        compiler_params=_cparams(("parallel", "arbitrary")),
        name="in_proj_gate",
    )(x, g1, w)


def _in_proj_qkv_kernel(x_ref, g1_ref, w_ref, gqk_ref, cos_ref, sin_ref, o_ref, rows_ref, h_ref):
    j = pl.program_id(1)

    @pl.when(j == 0)
    def _():
        h_ref[...] = (_rms(x_ref[...]) * g1_ref[...]).astype(BF16)

    tm = x_ref.shape[0]
    rc = min(tm, ROW_CHUNK)

    def tile(rope, rows):
        for c in range(tm // rc):
            rs = slice(c * rc, (c + 1) * rc)
            res = jnp.dot(h_ref[rs, :], w_ref[...], preferred_element_type=F32)
            for h in range(ATTN_HEADS):
                hs = slice(h * HEAD_DIM, (h + 1) * HEAD_DIM)
                y = res[:, hs]
                if rope:
                    y = _rms(y) * gqk_ref[:, hs]
                    y = y * cos_ref[rs, :] + pltpu.roll(y, HEAD_DIM // 2, 1) * sin_ref[rs, :]
                o_ref[rs, hs] = y
                if rows:
                    rows_ref[pl.ds(c * rc * ATTN_HEADS + h, rc, stride=ATTN_HEADS), :] = y

    pl.when(j < N_GROUPS)(lambda: tile(True, False))
    pl.when(jnp.logical_and(j >= N_GROUPS, j < 2 * N_GROUPS))(lambda: tile(True, True))
    pl.when(j >= 2 * N_GROUPS)(lambda: tile(False, True))


def _in_proj_qkv(x, g1, w, gqk, cos, sin, tm):
    m = x.shape[0]
    n_pos_blocks = cos.shape[0] // tm
    return pl.pallas_call(
        _in_proj_qkv_kernel,
        grid=(m // tm, 3 * N_GROUPS),
        in_specs=[
            pl.BlockSpec((tm, D_MODEL), lambda i, j: (i, 0)),
            pl.BlockSpec((1, D_MODEL), lambda i, j: (0, 0)),
            pl.BlockSpec((D_MODEL, QKV_TILE), lambda i, j: (0, j)),
            pl.BlockSpec((1, QKV_TILE), lambda i, j: (0, jnp.minimum(j, 2 * N_GROUPS - 1))),
            pl.BlockSpec((tm, HEAD_DIM), lambda i, j: (i % n_pos_blocks, 0)),
            pl.BlockSpec((tm, HEAD_DIM), lambda i, j: (i % n_pos_blocks, 0)),
        ],
        out_specs=[
            pl.BlockSpec((tm, QKV_TILE), lambda i, j: (i, j)),
            pl.BlockSpec((None, tm * ATTN_HEADS, HEAD_DIM),
                         lambda i, j: (jnp.maximum(j - N_GROUPS, 0), i, 0)),
        ],
        out_shape=[jax.ShapeDtypeStruct((m, 3 * QKV_WIDTH), F32),
                   jax.ShapeDtypeStruct((2 * N_GROUPS, m * ATTN_HEADS, HEAD_DIM), F32)],
        scratch_shapes=[pltpu.VMEM((tm, D_MODEL), BF16)],
        compiler_params=_cparams(("parallel", "arbitrary")),
        name="in_proj_qkv",
    )(x, g1, w, gqk, cos, sin)


def _attn_prompt_kernel(q0, q1, q2, k0, k1, k2, v0, v1, v2, o_ref, m_sc, l_sc, a_sc):
    seq = o_ref.shape[0]
    nk = ATTN_NK
    row = lax.broadcasted_iota(jnp.int32, (nk, nk), 0)
    col = lax.broadcasted_iota(jnp.int32, (nk, nk), 1)
    mask_cur = col <= row
    mask_prev = col >= row

    def rows(start, dil):
        if dil == 1:
            return pl.ds(start, nk)
        return pl.ds(start, nk, stride=dil)

    def blocks(g, qkv, dil, starts):
        q_ref, k_ref, v_ref = qkv
        loaded = []
        for start, prev_start in starts:
            sl = rows(start, dil)
            item = [sl, q_ref[sl, :], k_ref[sl, :], v_ref[sl, :]]
            if prev_start is not None:
                slp = rows(prev_start, dil)
                item += [k_ref[slp, :], v_ref[slp, :]]
            loaded.append(item)
        scored = []
        for item in loaded:
            q = item[1].astype(BF16)
            s = [jnp.where(mask_cur, _dot_nt(q, item[2].astype(BF16)) * ATTN_SCALE, NEG_BIG)]
            if len(item) > 4:
                s.append(jnp.where(mask_prev, _dot_nt(q, item[4].astype(BF16)) * ATTN_SCALE,
                                   NEG_BIG))
            scored.append(s)
        probs = []
        for s in scored:
            m = jnp.max(s[0] if len(s) == 1 else jnp.maximum(s[0], s[1]), axis=-1, keepdims=True)
            p = [jnp.exp(x - m) for x in s]
            l = jnp.sum(p[0] if len(s) == 1 else p[0] + p[1], axis=-1, keepdims=True)
            probs.append((m, l, [x.astype(BF16) for x in p]))
        for item, (m, l, p) in zip(loaded, probs):
            sl = item[0]
            acc = jnp.dot(p[0], item[3].astype(BF16), preferred_element_type=F32)
            if len(p) > 1:
                acc = acc + jnp.dot(p[1], item[5].astype(BF16), preferred_element_type=F32)
            m_sc[g, sl, :] = jnp.broadcast_to(m, (nk, HEAD_DIM))
            l_sc[g, sl, :] = jnp.broadcast_to(l, (nk, HEAD_DIM))
            a_sc[g, sl, :] = acc

    refs = ((q0, k0, v0), (q1, k1, v1), (q2, k2, v2))
    unr = ATTN_UNROLL
    for g, (_, dil) in enumerate(ATTN_GROUPS):
        qkv = refs[g]
        nb = seq // dil // nk
        stride = nk * dil
        if nb == 1:
            def body(it, carry, g=g, qkv=qkv, dil=dil):
                blocks(g, qkv, dil, [(it * unr + u, None) for u in range(unr)])
                return carry
            lax.fori_loop(0, dil // unr, body, 0)
        elif dil == 1:
            blocks(g, qkv, dil, [(n * nk, None if n == 0 else (n - 1) * nk) for n in range(unr)])

            def body(it, carry, g=g, qkv=qkv, dil=dil):
                starts = [pl.multiple_of((it * unr + u) * nk, nk) for u in range(unr)]
                blocks(g, qkv, dil, [(s, s - nk) for s in starts])
                return carry
            lax.fori_loop(1, nb // unr, body, 0)
        else:
            blocks(g, qkv, dil, [(r, None) for r in range(dil)])

            def body(n, carry, g=g, qkv=qkv, dil=dil, stride=stride):
                blocks(g, qkv, dil, [(r + n * stride, r + (n - 1) * stride) for r in range(dil)])
                return carry
            lax.fori_loop(1, nb, body, 0)

    m_all = jnp.maximum(jnp.maximum(m_sc[0], m_sc[1]), m_sc[2])
    num = jnp.zeros((seq, HEAD_DIM), F32)
    den = jnp.zeros((seq, HEAD_DIM), F32)
    for g in range(N_GROUPS):
        w = jnp.exp(m_sc[g] - m_all)
        num = num + w * a_sc[g]
        den = den + w * l_sc[g]
    o_ref[...] = num / den


def _attn_prompt(qkv3):
    b, seq, _ = qkv3.shape

    def spec(col0, g):
        base = (col0 + g * GROUP_WIDTH) // HEAD_DIM
        return pl.BlockSpec((None, seq, HEAD_DIM), lambda bi, h, base=base: (bi, 0, base + h))

    in_specs = ([spec(COL_Q, g) for g in range(N_GROUPS)]
                + [spec(COL_K, g) for g in range(N_GROUPS)]
                + [spec(COL_V, g) for g in range(N_GROUPS)])
    return pl.pallas_call(
        _attn_prompt_kernel,
        grid=(b, ATTN_HEADS),
        in_specs=in_specs,
        out_specs=pl.BlockSpec((None, seq, HEAD_DIM), lambda bi, h: (bi, 0, h)),
        out_shape=jax.ShapeDtypeStruct((b, seq, GROUP_WIDTH), F32),
        scratch_shapes=[pltpu.VMEM((N_GROUPS, seq, HEAD_DIM), F32)] * 3,
        compiler_params=_cparams(("parallel", "parallel")),
        name="attn_prompt",
    )(*([qkv3] * 9))


def _attn_sample_kernel(qkv_ref, kc0, vc0, kc1, vc1, kc2, vc2, o_ref):
    bb = o_ref.shape[0]
    caches = ((kc0, vc0), (kc1, vc1), (kc2, vc2))
    n_rows = ATTN_NK * ATTN_HEADS
    r8 = lax.broadcasted_iota(jnp.int32, (8, n_rows), 0)
    c8 = lax.broadcasted_iota(jnp.int32, (8, n_rows), 1)
    own = (r8 & (ATTN_HEADS - 1)) == (c8 & (ATTN_HEADS - 1))

    def window(cache_ref, b):
        return cache_ref[b].reshape(n_rows, HEAD_DIM).astype(BF16)

    def heads_on_rows(row, col0):
        hs = [row[:, col0 + h * HEAD_DIM:col0 + (h + 1) * HEAD_DIM] for h in range(ATTN_HEADS)]
        return jnp.concatenate(hs + hs, axis=0)

    def body(it, carry):
        items = []
        for u in range(SAMPLE_UNROLL):
            b = it * SAMPLE_UNROLL + u
            row = qkv_ref[b]
            for g in range(N_GROUPS):
                lo = g * GROUP_WIDTH
                q = heads_on_rows(row, COL_Q + lo)
                kn = heads_on_rows(row, COL_K + lo)
                vn = heads_on_rows(row, COL_V + lo)
                s = _dot_nt(q.astype(BF16), window(caches[g][0], b)) * ATTN_SCALE
                s = jnp.where(own, s, NEG_BIG)
                sn = jnp.sum(q * kn, axis=-1, keepdims=True) * ATTN_SCALE
                items.append((b, g, s, sn, vn))
        probs = []
        for b, g, s, sn, vn in items:
            m = jnp.maximum(jnp.max(s, axis=-1, keepdims=True), sn)
            p = jnp.exp(s - m)
            pn = jnp.exp(sn - m)
            probs.append((m, jnp.sum(p, axis=-1, keepdims=True) + pn, p.astype(BF16), pn))
        accs = [jnp.dot(p, window(caches[g][1], b), preferred_element_type=F32) + pn * vn
                for (b, g, _, _, vn), (_, _, p, pn) in zip(items, probs)]
        for u in range(SAMPLE_UNROLL):
            sl = slice(u * N_GROUPS, (u + 1) * N_GROUPS)
            ms = [pr[0] for pr in probs[sl]]
            ls = [pr[1] for pr in probs[sl]]
            m_all = jnp.maximum(jnp.maximum(ms[0], ms[1]), ms[2])
            num = jnp.zeros((8, HEAD_DIM), F32)
            den = jnp.zeros((8, 1), F32)
            for g in range(N_GROUPS):
                w = jnp.exp(ms[g] - m_all)
                num = num + w * accs[sl][g]
                den = den + w * ls[g]
            out = num / den
            o_ref[it * SAMPLE_UNROLL + u] = jnp.concatenate(
                [out[h:h + 1] for h in range(ATTN_HEADS)], axis=1)
        return carry

    lax.fori_loop(0, bb // SAMPLE_UNROLL, body, 0)


def _attn_sample(qkv3, caches, layer, bb):
    m = qkv3.shape[0]
    cache_specs = [pl.BlockSpec((None, bb, ATTN_NK, None, ATTN_HEADS, HEAD_DIM),
                                lambda i: (layer, i, 0, 0, 0, 0))
                   for _ in range(2 * N_GROUPS)]
    return pl.pallas_call(
        _attn_sample_kernel,
        grid=(m // bb,),
        in_specs=[pl.BlockSpec((bb, 1, 3 * QKV_WIDTH), lambda i: (i, 0, 0))] + cache_specs,
        out_specs=pl.BlockSpec((bb, 1, GROUP_WIDTH), lambda i: (i, 0, 0)),
        out_shape=jax.ShapeDtypeStruct((m, 1, GROUP_WIDTH), F32),
        compiler_params=_cparams(("parallel",)),
        name="attn_sample",
    )(qkv3, *caches)


def _hgrn_gates(qh, fh, lb):
    fg = lb + (1.0 - lb) * jax.nn.sigmoid(fh)
    return jax.nn.silu(qh), fg


def _cumsum8(y, rid):
    for s in (1, 2, 4):
        y = y + jnp.where(rid >= s, pltpu.roll(y, s, 0), 0.0)
    return y


def _hgrn_prompt_kernel(qh_ref, fh_ref, ih_ref, gh_ref, lb_ref, gn_ref, o_ref, sfin_ref, st_sc):
    t = pl.program_id(1)
    blk = HG_BLOCK
    c = HG_CHUNK
    nch = blk // c

    @pl.when(t == 0)
    def _():
        st_sc[...] = jnp.zeros_like(st_sc)

    rowi = lax.broadcasted_iota(jnp.int32, (blk, blk), 0)
    coli = lax.broadcasted_iota(jnp.int32, (blk, blk), 1)
    causal = coli <= rowi
    rid8 = lax.broadcasted_iota(jnp.int32, (8, HG_DK), 0)

    def head_scores(qh, fh, lb):
        q, fg = _hgrn_gates(qh, fh, lb)
        lf = jnp.log(fg)
        kh = 1.0 - fg

        slabs = []
        for k in range(blk // 8):
            s8 = _cumsum8(lf[8 * k:8 * k + 8], rid8)
            if k % 2 == 1:
                s8 = s8 + slabs[k - 1][7:8, :]
            slabs.append(s8)
        g_pre = jnp.zeros((1, HG_DK), F32)
        g_starts, acum_slabs = [], []
        for i in range(nch):
            g_starts.append(g_pre)
            acum_slabs += [slabs[2 * i] + g_pre, slabs[2 * i + 1] + g_pre]
            g_pre = g_pre + slabs[2 * i + 1][7:8, :]
        g_end = g_pre
        a = jnp.concatenate(slabs, axis=0)
        acum = jnp.concatenate(acum_slabs, axis=0)

        qt = (q * jnp.exp(a)).astype(BF16)
        qhat = (q * jnp.exp(acum)).astype(BF16)
        khat = (kh * jnp.exp(g_end - acum)).astype(BF16)

        prow = []
        for i in range(nch):
            n = c * (i + 1)
            kt = (kh[:n] * jnp.exp(g_starts[i] - acum[:n])).astype(BF16)
            if n < blk:
                kt = jnp.concatenate([kt, jnp.zeros((blk - n, HG_DK), BF16)], axis=0)
            prow.append(_dot_nt(qt[c * i:c * (i + 1)], kt))
        return prow, qhat, khat, jnp.exp(g_end)

    def head_output(scores, v, gh, st0):
        prow, qhat, khat, dec = scores
        p = jnp.where(causal, jnp.concatenate(prow, axis=0), 0.0).astype(BF16)
        vt = v.T.astype(BF16)
        o = _dot_nt(jnp.concatenate([p, qhat], axis=1),
                    jnp.concatenate([vt, st0.astype(BF16)], axis=1))
        o = _rms(o) * gn_ref[...] * jax.nn.sigmoid(gh)
        return o, st0 * dec + jnp.dot(vt, khat, preferred_element_type=F32)

    def head_group(i, carry):
        heads = [i * HG_HEAD_UNROLL + u for u in range(HG_HEAD_UNROLL)]
        cols = [pl.ds(pl.multiple_of(h * HG_DK, HG_DK), HG_DK) for h in heads]
        loaded = [(qh_ref[:, cs], fh_ref[:, cs], lb_ref[:, cs], ih_ref[:, cs], gh_ref[:, cs],
                   st_sc[h]) for h, cs in zip(heads, cols)]
        scores = [head_scores(*args[:3]) for args in loaded]
        results = [head_output(sc, *args[3:]) for sc, args in zip(scores, loaded)]
        for h, cs, (o, st_new) in zip(heads, cols, results):
            o_ref[:, cs] = o
            st_sc[h] = st_new
        return carry

    lax.fori_loop(0, HG_HEADS // HG_HEAD_UNROLL, head_group, 0)

    @pl.when(t == pl.num_programs(1) - 1)
    def _():
        for h in range(HG_HEADS):
            sfin_ref[h] = st_sc[h].T


def _hgrn_prompt(gate3, lb, gn):
    b, seq, _ = gate3.shape

    def seg(col0):
        blk = col0 // HG_WIDTH
        return pl.BlockSpec((None, HG_BLOCK, HG_WIDTH), lambda bi, t, blk=blk: (bi, t, blk))

    return pl.pallas_call(
        _hgrn_prompt_kernel,
        grid=(b, seq // HG_BLOCK),
        in_specs=[seg(COL_QH), seg(COL_FH), seg(COL_IH), seg(COL_GH),
                  pl.BlockSpec((1, HG_WIDTH), lambda bi, t: (0, 0)),
                  pl.BlockSpec((1, HG_DV), lambda bi, t: (0, 0))],
        out_specs=[pl.BlockSpec((None, HG_BLOCK, HG_WIDTH), lambda bi, t: (bi, t, 0)),
                   pl.BlockSpec((None, HG_HEADS, HG_DK, HG_DV), lambda bi, t: (bi, 0, 0, 0))],
        out_shape=[jax.ShapeDtypeStruct((b, seq, HG_WIDTH), F32),
                   jax.ShapeDtypeStruct((b, HG_HEADS, HG_DK, HG_DV), F32)],
        scratch_shapes=[pltpu.VMEM((HG_HEADS, HG_DV, HG_DK), F32)],
        compiler_params=_cparams(("parallel", "arbitrary")),
        name="hgrn_prompt",
    )(gate3, gate3, gate3, gate3, lb, gn)


def _hgrn_sample_kernel(qh_ref, fh_ref, ih_ref, gh_ref, lb_ref, gn_ref, s_ref, *rest):
    o_ref, sout_ref = rest[-2:]
    bb = qh_ref.shape[0]
    lb = lb_ref[...]
    ri = lax.broadcasted_iota(jnp.int32, (HG_DK, HG_DK), 0)
    ci = lax.broadcasted_iota(jnp.int32, (HG_DK, HG_DK), 1)
    diag = ri == ci

    def column(rowvec):
        full = jnp.broadcast_to(rowvec, (HG_DK, HG_DK))
        return jnp.sum(jnp.where(diag, full, 0.0), axis=1, keepdims=True)

    def body(b, carry):
        q, fg = _hgrn_gates(qh_ref[b], fh_ref[b], lb)
        v = ih_ref[b]
        gate = jax.nn.sigmoid(gh_ref[b])
        a = jnp.log(fg)
        kh = 1.0 - fg
        qt = q * jnp.exp(a)
        kt = kh * jnp.exp(-a)
        dec = jnp.exp(a)
        for h in range(HG_HEADS):
            hs = slice(h * HG_DK, (h + 1) * HG_DK)
            s0 = s_ref[b, h]
            att = jnp.sum(qt[:, hs] * kt[:, hs], axis=-1, keepdims=True)
            q8 = jnp.broadcast_to(qt[:, hs], (8, HG_DK)).astype(BF16)
            o = att * v[:, hs] + jnp.dot(q8, s0.astype(BF16), preferred_element_type=F32)[0:1]
            sout_ref[b, h] = column(dec[:, hs]) * s0 + column(kh[:, hs]) * v[:, hs]
            o_ref[b, :, hs] = _rms(o) * gn_ref[...] * gate[:, hs]
        return carry

    lax.fori_loop(0, bb, body, 0)


def _hgrn_sample(gate3, lb, gn, state, layer, bb, stacked):
    m = gate3.shape[0]
    prev_specs = [] if stacked is None else [pl.BlockSpec(memory_space=pl.ANY)]
    prev_args = [] if stacked is None else [stacked]
    n_in = 7

    def seg(col0):
        blk = col0 // HG_WIDTH
        return pl.BlockSpec((bb, 1, HG_WIDTH), lambda i, blk=blk: (i, 0, blk))

    return pl.pallas_call(
        _hgrn_sample_kernel,
        grid=(m // bb,),
        in_specs=[seg(COL_QH), seg(COL_FH), seg(COL_IH), seg(COL_GH),
                  pl.BlockSpec((1, HG_WIDTH), lambda i: (0, 0)),
                  pl.BlockSpec((1, HG_DV), lambda i: (0, 0)),
                  pl.BlockSpec((None, bb, HG_HEADS, HG_DK, HG_DV),
                               lambda i: (layer, i, 0, 0, 0))] + prev_specs,
        out_specs=[pl.BlockSpec((bb, 1, HG_WIDTH), lambda i: (i, 0, 0)),
                   pl.BlockSpec((None, bb, HG_HEADS, HG_DK, HG_DV),
                                lambda i: (layer, i, 0, 0, 0))],
        out_shape=[jax.ShapeDtypeStruct((m, 1, HG_WIDTH), F32),
                   jax.ShapeDtypeStruct((DEPTH, m, HG_HEADS, HG_DK, HG_DV), F32)],
        input_output_aliases={} if stacked is None else {n_in: 1},
        compiler_params=_cparams(("parallel",)),
        name="hgrn_sample",
    )(gate3, gate3, gate3, gate3, lb, gn, state, *prev_args)


def _merge_kernel(x_ref, ya_ref, yb_ref, ga_ref, gb_ref, wpa_ref, wpb_ref, wo_ref, o_ref):
    tm = x_ref.shape[0]
    rc = min(tm, ROW_CHUNK)

    def branches(c):
        rs = slice(c * rc, (c + 1) * rc)
        return (jnp.dot(ya_ref[rs, :].astype(BF16), wpa_ref[...], preferred_element_type=F32),
                jnp.dot(yb_ref[rs, :].astype(BF16), wpb_ref[...], preferred_element_type=F32))

    nxt = branches(0)
    for c in range(tm // rc):
        rs = slice(c * rc, (c + 1) * rc)
        pa, pb = nxt
        if c + 1 < tm // rc:
            nxt = branches(c + 1)
        merged = jax.nn.sigmoid(ga_ref[rs, :]) * pa + jax.nn.sigmoid(gb_ref[rs, :]) * pb
        o_ref[rs, :] = x_ref[rs, :] + jnp.dot(merged.astype(BF16), wo_ref[...],
                                              preferred_element_type=F32)


def _merge(x, ya, yb, gate, wpa, wpb, wo, tm):
    m = x.shape[0]
    row = lambda w: pl.BlockSpec((tm, w), lambda i: (i, 0))
    full = lambda a: pl.BlockSpec(a.shape, lambda i: (0, 0))
    return pl.pallas_call(
        _merge_kernel,
        grid=(m // tm,),
        in_specs=[row(D_MODEL), row(GROUP_WIDTH), row(HG_WIDTH),
                  pl.BlockSpec((tm, D_MODEL), lambda i: (i, COL_GA // D_MODEL)),
                  pl.BlockSpec((tm, D_MODEL), lambda i: (i, COL_GB // D_MODEL)),
                  full(wpa), full(wpb), full(wo)],
        out_specs=row(D_MODEL),
        out_shape=jax.ShapeDtypeStruct((m, D_MODEL), F32),
        compiler_params=_cparams(("parallel",)),
        name="merge",
    )(x, ya, yb, gate, gate, wpa, wpb, wo)


def _mlp_kernel(x_ref, g2_ref, wup_ref, wdn_ref, o_ref, h_ref):
    f = pl.program_id(1)

    @pl.when(f == 0)
    def _():
        x = x_ref[...]
        h_ref[...] = (_rms(x) * g2_ref[...]).astype(BF16)
        o_ref[...] = x

    tm = x_ref.shape[0]
    rc = min(tm, ROW_CHUNK)

    def up(c):
        return jnp.dot(h_ref[c * rc:(c + 1) * rc, :], wup_ref[...], preferred_element_type=F32)

    nxt = up(0)
    for c in range(tm // rc):
        rs = slice(c * rc, (c + 1) * rc)
        u = nxt
        if c + 1 < tm // rc:
            nxt = up(c + 1)
        act = jnp.square(jnp.maximum(u, 0.0)).astype(BF16)
        o_ref[rs, :] += jnp.dot(act, wdn_ref[...], preferred_element_type=F32)


def _mlp(x, g2, wup, wdn, tm, tf):
    m = x.shape[0]
    return pl.pallas_call(
        _mlp_kernel,
        grid=(m // tm, D_FF // tf),
        in_specs=[pl.BlockSpec((tm, D_MODEL), lambda i, f: (i, 0)),
                  pl.BlockSpec((1, D_MODEL), lambda i, f: (0, 0)),
                  pl.BlockSpec((D_MODEL, tf), lambda i, f: (0, f)),
                  pl.BlockSpec((tf, D_MODEL), lambda i, f: (f, 0))],
        out_specs=pl.BlockSpec((tm, D_MODEL), lambda i, f: (i, 0)),
        out_shape=jax.ShapeDtypeStruct((m, D_MODEL), F32),
        scratch_shapes=[pltpu.VMEM((tm, D_MODEL), BF16)],
        compiler_params=_cparams(("parallel", "arbitrary")),
        name="mlp",
    )(x, g2, wup, wdn)


def _rope_tables(pos):
    half = HEAD_DIM // 2
    inv = ROPE_THETA ** (-(jnp.arange(half, dtype=F32) * 2.0 / HEAD_DIM))
    ang = pos.astype(F32)[:, None] * inv[None, :]
    cos, sin = jnp.cos(ang), jnp.sin(ang)
    return jnp.concatenate([cos, cos], axis=-1), jnp.concatenate([-sin, sin], axis=-1)


def kernel(x_prompt, x_sample, cache_k_w128, cache_v_w128, cache_k_w512, cache_v_w512,
           cache_k_w2048, cache_v_w2048, state_hgrn, w_in, norm1, q_norm, k_norm, hg_out_norm,
           hg_lower_bounds, w_pa, w_pb, w_o, norm2, w_up, w_down):
    bsz, seq, _ = x_prompt.shape
    dec_b, dec_t, _ = x_sample.shape
    mp = bsz * seq
    ms = dec_b * dec_t

    lb_all = jnp.cumsum(jax.nn.softmax(hg_lower_bounds.astype(F32), axis=0), axis=0)
    lb_all = lb_all - lb_all[0:1]

    cos_p, sin_p = _rope_tables(jnp.arange(seq))
    cos_s, sin_s = _rope_tables(jnp.broadcast_to(PAST_LEN + jnp.arange(dec_t), (ms,)))

    caches = []
    for (win, dil), (ck, cv) in zip(ATTN_GROUPS, ((cache_k_w128, cache_v_w128),
                                                   (cache_k_w512, cache_v_w512),
                                                   (cache_k_w2048, cache_v_w2048))):
        wc = ck.shape[2]
        shape = (DEPTH, dec_b, wc // dil, dil, ATTN_HEADS, HEAD_DIM)
        caches += [ck.reshape(shape), cv.reshape(shape)]

    w_in_b, w_pa_b, w_pb_b, w_o_b, w_up_b, w_dn_b = (
        w.astype(BF16) for w in (w_in, w_pa, w_pb, w_o, w_up, w_down))

    yp = x_prompt.reshape(mp, D_MODEL)
    ys = x_sample.reshape(ms, D_MODEL)
    p_rows = [[] for _ in range(2 * N_GROUPS)]
    s_rows = [[] for _ in range(2 * N_GROUPS)]
    p_states, s_states = [], None

    for layer in range(DEPTH):
        w_in_l = w_in_b[layer]
        wpa, wpb, wo = w_pa_b[layer], w_pb_b[layer], w_o_b[layer]
        wup, wdn = w_up_b[layer], w_dn_b[layer]
        g1 = norm1[layer].reshape(1, D_MODEL)
        g2 = norm2[layer].reshape(1, D_MODEL)
        gqk = jnp.concatenate([jnp.tile(q_norm[layer], (1, ATTN_HEADS)).reshape(1, QKV_WIDTH),
                               jnp.tile(k_norm[layer], (1, ATTN_HEADS)).reshape(1, QKV_WIDTH)],
                              axis=-1)
        gn = hg_out_norm[layer].reshape(1, HG_DV)
        lb = lb_all[layer].reshape(1, HG_WIDTH)

        gate = _in_proj_gate(yp, g1, w_in_l, tm=1024)
        qkv, rows = _in_proj_qkv(yp, g1, w_in_l, gqk, cos_p, sin_p, tm=1024)
        ya = _attn_prompt(qkv.reshape(bsz, seq, 3 * QKV_WIDTH))
        yb, st = _hgrn_prompt(gate.reshape(bsz, seq, GATE_WIDTH), lb, gn)
        yp = _merge(yp, ya.reshape(mp, GROUP_WIDTH), yb.reshape(mp, HG_WIDTH), gate,
                    wpa, wpb, wo, tm=512)
        yp = _mlp(yp, g2, wup, wdn, tm=512, tf=1024)
        rows5 = rows.reshape(2 * N_GROUPS, bsz, seq, ATTN_HEADS, HEAD_DIM)
        for g, (win, _) in enumerate(ATTN_GROUPS):
            keep = min(win, seq)
            for c in range(2):
                p_rows[2 * g + c].append(rows5[c * N_GROUPS + g, :, seq - keep:])
        p_states.append(st)

        gate_s = _in_proj_gate(ys, g1, w_in_l, tm=ms)
        qkv_s, rows_s = _in_proj_qkv(ys, g1, w_in_l, gqk, cos_s, sin_s, tm=ms)
        ya_s = _attn_sample(qkv_s.reshape(ms, 1, 3 * QKV_WIDTH), caches, layer, bb=8)
        yb_s, s_states = _hgrn_sample(gate_s.reshape(ms, 1, GATE_WIDTH), lb, gn, state_hgrn,
                                      layer, bb=8, stacked=s_states)
        ys = _merge(ys, ya_s.reshape(ms, GROUP_WIDTH), yb_s.reshape(ms, HG_WIDTH), gate_s,
                    wpa, wpb, wo, tm=ms)
        ys = _mlp(ys, g2, wup, wdn, tm=ms, tf=1024)
        rows5_s = rows_s.reshape(2 * N_GROUPS, dec_b, dec_t, ATTN_HEADS, HEAD_DIM)
        for g in range(N_GROUPS):
            for c in range(2):
                s_rows[2 * g + c].append(rows5_s[c * N_GROUPS + g])

    outs = [yp.reshape(bsz, seq, D_MODEL), ys.reshape(dec_b, dec_t, D_MODEL)]
    outs += [jnp.stack(r, axis=0) for r in p_rows]
    outs.append(jnp.stack(p_states, axis=0))
    outs += [jnp.stack(r, axis=0) for r in s_rows]
    outs.append(s_states)
    return tuple(outs)
```

```python
import functools

import jax
import jax.numpy as jnp
from jax import lax
from jax.experimental import pallas as pl
from jax.experimental.pallas import tpu as pltpu

F32 = jnp.float32
BF16 = jnp.bfloat16

D_MODEL = 1024
DEPTH = 4
PAST_LEN = 2048
ATTN_GROUPS = ((128, 1), (512, 4), (2048, 16))
N_GROUPS = 3
ATTN_HEADS = 4
HEAD_DIM = 128
GROUP_WIDTH = ATTN_HEADS * HEAD_DIM
QKV_WIDTH = N_GROUPS * GROUP_WIDTH
ATTN_NK = 128
ATTN_SCALE = HEAD_DIM ** -0.5
HG_HEADS = 8
HG_DK = 128
HG_DV = 128
HG_WIDTH = 1024
HG_CHUNK = 16
HG_BLOCK = 128
HG_HEAD_UNROLL = 4
D_FF = 4 * D_MODEL
ROPE_THETA = 10000.0
NORM_EPS = 1e-6
IN_WIDTH = 3 * QKV_WIDTH + 6 * HG_WIDTH
GATE_WIDTH = 6 * HG_WIDTH
NEG_BIG = -1e30

COL_QH, COL_FH, COL_IH, COL_GH, COL_GA, COL_GB = (i * 1024 for i in range(6))
COL_Q, COL_K, COL_V = 0, QKV_WIDTH, 2 * QKV_WIDTH

GATE_TILE = 1536
ROW_CHUNK = 256
ATTN_UNROLL = 4
SAMPLE_UNROLL = 2

VMEM_LIMIT = 48 * 1024 * 1024


def _cparams(sem):
    return pltpu.CompilerParams(dimension_semantics=sem, vmem_limit_bytes=VMEM_LIMIT)


def _rms(x):
    return x * lax.rsqrt(jnp.mean(x * x, axis=-1, keepdims=True) + NORM_EPS)


def _dot_nt(a, b):
    return lax.dot_general(a, b, (((1,), (1,)), ((), ())), preferred_element_type=F32)


def _in_proj_gate_kernel(x_ref, g1_ref, w_ref, o_ref, h_ref):
    @pl.when(pl.program_id(1) == 0)
    def _():
        h_ref[...] = (_rms(x_ref[...]) * g1_ref[...]).astype(BF16)

    tm = x_ref.shape[0]
    rc = min(tm, ROW_CHUNK)
    for c in range(tm // rc):
        rs = slice(c * rc, (c + 1) * rc)
        o_ref[rs, :] = jnp.dot(h_ref[rs, :], w_ref[...], preferred_element_type=F32)


def _in_proj_gate(x, g1, w, tm):
    m = x.shape[0]
    first = 3 * QKV_WIDTH // GATE_TILE
    return pl.pallas_call(
        _in_proj_gate_kernel,
        grid=(m // tm, GATE_WIDTH // GATE_TILE),
        in_specs=[
            pl.BlockSpec((tm, D_MODEL), lambda i, j: (i, 0)),
            pl.BlockSpec((1, D_MODEL), lambda i, j: (0, 0)),
            pl.BlockSpec((D_MODEL, GATE_TILE), lambda i, j: (0, first + j)),
        ],
        out_specs=pl.BlockSpec((tm, GATE_TILE), lambda i, j: (i, j)),
        out_shape=jax.ShapeDtypeStruct((m, GATE_WIDTH), F32),
        scratch_shapes=[pltpu.VMEM((tm, D_MODEL), BF16)],
        compiler_params=_cparams(("parallel", "arbitrary")),
        name="in_proj_gate",
    )(x, g1, w)


def _in_proj_qkv_kernel(keep_in_tile, x_ref, g1_ref, w_ref, gqk_ref, cos_ref, sin_ref, *rest):
    o_ref, h_ref = rest[-8], rest[-1]
    k_rows, v_rows = rest[-7:-4], rest[-4:-1]
    j = pl.program_id(1)

    @pl.when(j == 0)
    def _():
        h_ref[...] = (_rms(x_ref[...]) * g1_ref[...]).astype(BF16)

    tm = x_ref.shape[0]
    rc = min(tm, ROW_CHUNK)

    def tile(rope, row_refs):
        for c in range(tm // rc):
            rs = slice(c * rc, (c + 1) * rc)
            for g in range(N_GROUPS):
                res = jnp.dot(h_ref[rs, :], w_ref[:, g * GROUP_WIDTH:(g + 1) * GROUP_WIDTH],
                              preferred_element_type=F32)
                first_kept = tm - keep_in_tile[g]
                lo = max(c * rc, first_kept)
                for h in range(ATTN_HEADS):
                    cs = slice(g * GROUP_WIDTH + h * HEAD_DIM, g * GROUP_WIDTH + (h + 1) * HEAD_DIM)
                    y = res[:, h * HEAD_DIM:(h + 1) * HEAD_DIM]
                    if rope:
                        y = _rms(y) * gqk_ref[:, cs]
                        y = y * cos_ref[rs, :] + pltpu.roll(y, HEAD_DIM // 2, 1) * sin_ref[rs, :]
                    o_ref[rs, cs] = y
                    if row_refs is not None and lo < (c + 1) * rc:
                        n = (c + 1) * rc - lo
                        dst = (lo - first_kept) * ATTN_HEADS + h
                        row_refs[g][pl.ds(dst, n, stride=ATTN_HEADS), :] = y[lo - c * rc:, :]

    pl.when(j == 0)(lambda: tile(True, None))
    pl.when(j == 1)(lambda: tile(True, k_rows))
    pl.when(j == 2)(lambda: tile(False, v_rows))


def _in_proj_qkv(x, g1, w, gqk, cos, sin, tm, layer, n_seq, keep, stacked):
    m = x.shape[0]
    n_pos_blocks = cos.shape[0] // tm
    tiles_per_seq = m // n_seq // tm
    keep_in_tile = tuple(min(k, tm) for k in keep)
    row_specs, row_shapes = [], []
    for k, kt in zip(keep, keep_in_tile):
        first_tile = tiles_per_seq - k // kt

        def index(i, j, first_tile=first_tile):
            return (layer, i // tiles_per_seq, jnp.maximum(i % tiles_per_seq - first_tile, 0), 0)

        row_specs.append(pl.BlockSpec((None, None, kt * ATTN_HEADS, HEAD_DIM), index))
        row_shapes.append(jax.ShapeDtypeStruct((DEPTH, n_seq, k * ATTN_HEADS, HEAD_DIM), F32))
    n_in = 6
    prev_specs = [] if stacked is None else [pl.BlockSpec(memory_space=pl.ANY)] * 6
    prev_args = [] if stacked is None else list(stacked)
    aliases = {} if stacked is None else {n_in + r: 1 + r for r in range(6)}
    outs = pl.pallas_call(
        functools.partial(_in_proj_qkv_kernel, keep_in_tile),
        grid=(m // tm, 3),
        in_specs=[
            pl.BlockSpec((tm, D_MODEL), lambda i, j: (i, 0)),
            pl.BlockSpec((1, D_MODEL), lambda i, j: (0, 0)),
            pl.BlockSpec((D_MODEL, QKV_WIDTH), lambda i, j: (0, j)),
            pl.BlockSpec((1, QKV_WIDTH), lambda i, j: (0, jnp.minimum(j, 1))),
            pl.BlockSpec((tm, HEAD_DIM), lambda i, j: (i % n_pos_blocks, 0)),
            pl.BlockSpec((tm, HEAD_DIM), lambda i, j: (i % n_pos_blocks, 0)),
        ] + prev_specs,
        out_specs=[pl.BlockSpec((tm, QKV_WIDTH), lambda i, j: (i, j))] + row_specs * 2,
        out_shape=[jax.ShapeDtypeStruct((m, 3 * QKV_WIDTH), F32)] + row_shapes * 2,
        input_output_aliases=aliases,
        scratch_shapes=[pltpu.VMEM((tm, D_MODEL), BF16)],
        compiler_params=_cparams(("arbitrary", "arbitrary")),
        name="in_proj_qkv",
    )(x, g1, w, gqk, cos, sin, *prev_args)
    return outs[0], outs[1:]


def _attn_prompt_kernel(q0, q1, q2, k0, k1, k2, v0, v1, v2, o_ref, m_sc, l_sc, a_sc):
    seq = o_ref.shape[0]
    nk = ATTN_NK
    row = lax.broadcasted_iota(jnp.int32, (nk, nk), 0)
    col = lax.broadcasted_iota(jnp.int32, (nk, nk), 1)
    mask_cur = col <= row
    mask_prev = col >= row

    def rows(start, dil):
        if dil == 1:
            return pl.ds(start, nk)
        return pl.ds(start, nk, stride=dil)

    def blocks(g, qkv, dil, starts):
        q_ref, k_ref, v_ref = qkv
        loaded = []
        for start, prev_start in starts:
            sl = rows(start, dil)
            item = [sl, q_ref[sl, :], k_ref[sl, :], v_ref[sl, :]]
            if prev_start is not None:
                slp = rows(prev_start, dil)
                item += [k_ref[slp, :], v_ref[slp, :]]
            loaded.append(item)
        scored = []
        for item in loaded:
            q = item[1].astype(BF16)
            s = [jnp.where(mask_cur, _dot_nt(q, item[2].astype(BF16)) * ATTN_SCALE, NEG_BIG)]
            if len(item) > 4:
                s.append(jnp.where(mask_prev, _dot_nt(q, item[4].astype(BF16)) * ATTN_SCALE,
                                   NEG_BIG))
            scored.append(s)
        probs = []
        for s in scored:
            m = jnp.max(s[0] if len(s) == 1 else jnp.maximum(s[0], s[1]), axis=-1, keepdims=True)
            p = [jnp.exp(x - m) for x in s]
            l = jnp.sum(p[0] if len(s) == 1 else p[0] + p[1], axis=-1, keepdims=True)
            probs.append((m, l, [x.astype(BF16) for x in p]))
        for item, (m, l, p) in zip(loaded, probs):
            sl = item[0]
            acc = jnp.dot(p[0], item[3].astype(BF16), preferred_element_type=F32)
            if len(p) > 1:
                acc = acc + jnp.dot(p[1], item[5].astype(BF16), preferred_element_type=F32)
            m_sc[g, sl, :] = jnp.broadcast_to(m, (nk, HEAD_DIM))
            l_sc[g, sl, :] = jnp.broadcast_to(l, (nk, HEAD_DIM))
            a_sc[g, sl, :] = acc

    refs = ((q0, k0, v0), (q1, k1, v1), (q2, k2, v2))
    unr = ATTN_UNROLL
    for g, (_, dil) in enumerate(ATTN_GROUPS):
        qkv = refs[g]
        nb = seq // dil // nk
        stride = nk * dil
        if nb == 1:
            def body(it, carry, g=g, qkv=qkv, dil=dil):
                blocks(g, qkv, dil, [(it * unr + u, None) for u in range(unr)])
                return carry
            lax.fori_loop(0, dil // unr, body, 0)
        elif dil == 1:
            blocks(g, qkv, dil, [(n * nk, None if n == 0 else (n - 1) * nk) for n in range(unr)])

            def body(it, carry, g=g, qkv=qkv, dil=dil):
                starts = [pl.multiple_of((it * unr + u) * nk, nk) for u in range(unr)]
                blocks(g, qkv, dil, [(s, s - nk) for s in starts])
                return carry
            lax.fori_loop(1, nb // unr, body, 0)
        else:
            blocks(g, qkv, dil, [(r, None) for r in range(dil)])

            def body(n, carry, g=g, qkv=qkv, dil=dil, stride=stride):
                blocks(g, qkv, dil, [(r + n * stride, r + (n - 1) * stride) for r in range(dil)])
                return carry
            lax.fori_loop(1, nb, body, 0)

    m_all = jnp.maximum(jnp.maximum(m_sc[0], m_sc[1]), m_sc[2])
    num = jnp.zeros((seq, HEAD_DIM), F32)
    den = jnp.zeros((seq, HEAD_DIM), F32)
    for g in range(N_GROUPS):
        w = jnp.exp(m_sc[g] - m_all)
        num = num + w * a_sc[g]
        den = den + w * l_sc[g]
    o_ref[...] = (num / den).astype(o_ref.dtype)


def _attn_prompt(qkv3):
    b, seq, _ = qkv3.shape

    def spec(col0, g):
        base = (col0 + g * GROUP_WIDTH) // HEAD_DIM
        return pl.BlockSpec((None, seq, HEAD_DIM), lambda bi, h, base=base: (bi, 0, base + h))

    in_specs = ([spec(COL_Q, g) for g in range(N_GROUPS)]
                + [spec(COL_K, g) for g in range(N_GROUPS)]
                + [spec(COL_V, g) for g in range(N_GROUPS)])
    return pl.pallas_call(
        _attn_prompt_kernel,
        grid=(b, ATTN_HEADS),
        in_specs=in_specs,
        out_specs=pl.BlockSpec((None, seq, HEAD_DIM), lambda bi, h: (bi, 0, h)),
        out_shape=jax.ShapeDtypeStruct((b, seq, GROUP_WIDTH), BF16),
        scratch_shapes=[pltpu.VMEM((N_GROUPS, seq, HEAD_DIM), F32)] * 3,
        compiler_params=_cparams(("parallel", "parallel")),
        name="attn_prompt",
    )(*([qkv3] * 9))


def _attn_sample_kernel(qkv_ref, kc0, vc0, kc1, vc1, kc2, vc2, o_ref):
    bb = o_ref.shape[0]
    caches = ((kc0, vc0), (kc1, vc1), (kc2, vc2))
    n_rows = ATTN_NK * ATTN_HEADS
    r8 = lax.broadcasted_iota(jnp.int32, (8, n_rows), 0)
    c8 = lax.broadcasted_iota(jnp.int32, (8, n_rows), 1)
    own = (r8 & (ATTN_HEADS - 1)) == (c8 & (ATTN_HEADS - 1))

    def window(cache_ref, b):
        return cache_ref[b].reshape(n_rows, HEAD_DIM).astype(BF16)

    def heads_on_rows(row, col0):
        hs = [row[:, col0 + h * HEAD_DIM:col0 + (h + 1) * HEAD_DIM] for h in range(ATTN_HEADS)]
        return jnp.concatenate(hs + hs, axis=0)

    def body(it, carry):
        items = []
        for u in range(SAMPLE_UNROLL):
            b = it * SAMPLE_UNROLL + u
            row = qkv_ref[b]
            for g in range(N_GROUPS):
                lo = g * GROUP_WIDTH
                q = heads_on_rows(row, COL_Q + lo)
                kn = heads_on_rows(row, COL_K + lo)
                vn = heads_on_rows(row, COL_V + lo)
                s = _dot_nt(q.astype(BF16), window(caches[g][0], b)) * ATTN_SCALE
                s = jnp.where(own, s, NEG_BIG)
                sn = jnp.sum(q * kn, axis=-1, keepdims=True) * ATTN_SCALE
                items.append((b, g, s, sn, vn))
        probs = []
        for b, g, s, sn, vn in items:
            m = jnp.maximum(jnp.max(s, axis=-1, keepdims=True), sn)
            p = jnp.exp(s - m)
            pn = jnp.exp(sn - m)
            probs.append((m, jnp.sum(p, axis=-1, keepdims=True) + pn, p.astype(BF16), pn))
        accs = [jnp.dot(p, window(caches[g][1], b), preferred_element_type=F32) + pn * vn
                for (b, g, _, _, vn), (_, _, p, pn) in zip(items, probs)]
        for u in range(SAMPLE_UNROLL):
            sl = slice(u * N_GROUPS, (u + 1) * N_GROUPS)
            ms = [pr[0] for pr in probs[sl]]
            ls = [pr[1] for pr in probs[sl]]
            m_all = jnp.maximum(jnp.maximum(ms[0], ms[1]), ms[2])
            num = jnp.zeros((8, HEAD_DIM), F32)
            den = jnp.zeros((8, 1), F32)
            for g in range(N_GROUPS):
                w = jnp.exp(ms[g] - m_all)
                num = num + w * accs[sl][g]
                den = den + w * ls[g]
            out = num / den
            o_ref[it * SAMPLE_UNROLL + u] = jnp.concatenate(
                [out[h:h + 1] for h in range(ATTN_HEADS)], axis=1)
        return carry

    lax.fori_loop(0, bb // SAMPLE_UNROLL, body, 0)


def _attn_sample(qkv3, caches, layer, bb):
    m = qkv3.shape[0]
    cache_specs = [pl.BlockSpec((None, bb, ATTN_NK, None, ATTN_HEADS, HEAD_DIM),
                                lambda i: (layer, i, 0, 0, 0, 0))
                   for _ in range(2 * N_GROUPS)]
    return pl.pallas_call(
        _attn_sample_kernel,
        grid=(m // bb,),
        in_specs=[pl.BlockSpec((bb, 1, 3 * QKV_WIDTH), lambda i: (i, 0, 0))] + cache_specs,
        out_specs=pl.BlockSpec((bb, 1, GROUP_WIDTH), lambda i: (i, 0, 0)),
        out_shape=jax.ShapeDtypeStruct((m, 1, GROUP_WIDTH), F32),
        compiler_params=_cparams(("parallel",)),
        name="attn_sample",
    )(qkv3, *caches)


def _hgrn_gates(qh, fh, lb):
    fg = lb + (1.0 - lb) * jax.nn.sigmoid(fh)
    return jax.nn.silu(qh), fg


def _cumsum8(y, rid):
    for s in (1, 2, 4):
        y = y + jnp.where(rid >= s, pltpu.roll(y, s, 0), 0.0)
    return y


def _hgrn_prompt_kernel(qh_ref, fh_ref, ih_ref, gh_ref, lb_ref, gn_ref, o_ref, sfin_ref, st_sc):
    t = pl.program_id(1)
    blk = HG_BLOCK
    c = HG_CHUNK
    nch = blk // c

    @pl.when(t == 0)
    def _():
        st_sc[...] = jnp.zeros_like(st_sc)

    rowi = lax.broadcasted_iota(jnp.int32, (blk, blk), 0)
    coli = lax.broadcasted_iota(jnp.int32, (blk, blk), 1)
    causal = coli <= rowi
    rid8 = lax.broadcasted_iota(jnp.int32, (8, HG_DK), 0)

    def head_scores(qh, fh, lb):
        q, fg = _hgrn_gates(qh, fh, lb)
        lf = jnp.log(fg)
        kh = 1.0 - fg

        slabs = []
        for k in range(blk // 8):
            s8 = _cumsum8(lf[8 * k:8 * k + 8], rid8)
            if k % 2 == 1:
                s8 = s8 + slabs[k - 1][7:8, :]
            slabs.append(s8)
        g_pre = jnp.zeros((1, HG_DK), F32)
        g_starts, acum_slabs = [], []
        for i in range(nch):
            g_starts.append(g_pre)
            acum_slabs += [slabs[2 * i] + g_pre, slabs[2 * i + 1] + g_pre]
            g_pre = g_pre + slabs[2 * i + 1][7:8, :]
        g_end = g_pre
        a = jnp.concatenate(slabs, axis=0)
        acum = jnp.concatenate(acum_slabs, axis=0)

        qt = (q * jnp.exp(a)).astype(BF16)
        qhat = (q * jnp.exp(acum)).astype(BF16)
        khat = (kh * jnp.exp(g_end - acum)).astype(BF16)

        prow = []
        for i in range(nch):
            n = c * (i + 1)
            kt = (kh[:n] * jnp.exp(g_starts[i] - acum[:n])).astype(BF16)
            if n < blk:
                kt = jnp.concatenate([kt, jnp.zeros((blk - n, HG_DK), BF16)], axis=0)
            prow.append(_dot_nt(qt[c * i:c * (i + 1)], kt))
        return prow, qhat, khat, jnp.exp(g_end)

    def head_output(scores, v, gh, st0):
        prow, qhat, khat, dec = scores
        p = jnp.where(causal, jnp.concatenate(prow, axis=0), 0.0).astype(BF16)
        vt = v.T.astype(BF16)
        o = _dot_nt(jnp.concatenate([p, qhat], axis=1),
                    jnp.concatenate([vt, st0.astype(BF16)], axis=1))
        o = _rms(o) * gn_ref[...] * jax.nn.sigmoid(gh)
        return o, st0 * dec + jnp.dot(vt, khat, preferred_element_type=F32)

    def head_group(i, carry):
        heads = [i * HG_HEAD_UNROLL + u for u in range(HG_HEAD_UNROLL)]
        cols = [pl.ds(pl.multiple_of(h * HG_DK, HG_DK), HG_DK) for h in heads]
        loaded = [(qh_ref[:, cs], fh_ref[:, cs], lb_ref[:, cs], ih_ref[:, cs], gh_ref[:, cs],
                   st_sc[h]) for h, cs in zip(heads, cols)]
        scores = [head_scores(*args[:3]) for args in loaded]
        results = [head_output(sc, *args[3:]) for sc, args in zip(scores, loaded)]
        for h, cs, (o, st_new) in zip(heads, cols, results):
            o_ref[:, cs] = o.astype(o_ref.dtype)
            st_sc[h] = st_new
        return carry

    lax.fori_loop(0, HG_HEADS // HG_HEAD_UNROLL, head_group, 0)

    @pl.when(t == pl.num_programs(1) - 1)
    def _():
        for h in range(HG_HEADS):
            sfin_ref[h] = st_sc[h].T


def _hgrn_prompt(gate3, lb, gn):
    b, seq, _ = gate3.shape

    def seg(col0):
        blk = col0 // HG_WIDTH
        return pl.BlockSpec((None, HG_BLOCK, HG_WIDTH), lambda bi, t, blk=blk: (bi, t, blk))

    return pl.pallas_call(
        _hgrn_prompt_kernel,
        grid=(b, seq // HG_BLOCK),
        in_specs=[seg(COL_QH), seg(COL_FH), seg(COL_IH), seg(COL_GH),
                  pl.BlockSpec((1, HG_WIDTH), lambda bi, t: (0, 0)),
                  pl.BlockSpec((1, HG_DV), lambda bi, t: (0, 0))],
        out_specs=[pl.BlockSpec((None, HG_BLOCK, HG_WIDTH), lambda bi, t: (bi, t, 0)),
                   pl.BlockSpec((None, HG_HEADS, HG_DK, HG_DV), lambda bi, t: (bi, 0, 0, 0))],
        out_shape=[jax.ShapeDtypeStruct((b, seq, HG_WIDTH), BF16),
                   jax.ShapeDtypeStruct((b, HG_HEADS, HG_DK, HG_DV), F32)],
        scratch_shapes=[pltpu.VMEM((HG_HEADS, HG_DV, HG_DK), F32)],
        compiler_params=_cparams(("parallel", "arbitrary")),
        name="hgrn_prompt",
    )(gate3, gate3, gate3, gate3, lb, gn)


def _hgrn_sample_kernel(qh_ref, fh_ref, ih_ref, gh_ref, lb_ref, gn_ref, s_ref, *rest):
    o_ref, sout_ref = rest[-2:]
    bb = qh_ref.shape[0]
    lb = lb_ref[...]
    ri = lax.broadcasted_iota(jnp.int32, (HG_DK, HG_DK), 0)
    ci = lax.broadcasted_iota(jnp.int32, (HG_DK, HG_DK), 1)
    diag = ri == ci

    def column(rowvec):
        full = jnp.broadcast_to(rowvec, (HG_DK, HG_DK))
        return jnp.sum(jnp.where(diag, full, 0.0), axis=1, keepdims=True)

    def body(b, carry):
        q, fg = _hgrn_gates(qh_ref[b], fh_ref[b], lb)
        v = ih_ref[b]
        gate = jax.nn.sigmoid(gh_ref[b])
        a = jnp.log(fg)
        kh = 1.0 - fg
        qt = q * jnp.exp(a)
        kt = kh * jnp.exp(-a)
        dec = jnp.exp(a)
        for h in range(HG_HEADS):
            hs = slice(h * HG_DK, (h + 1) * HG_DK)
            s0 = s_ref[b, h]
            att = jnp.sum(qt[:, hs] * kt[:, hs], axis=-1, keepdims=True)
            q8 = jnp.broadcast_to(qt[:, hs], (8, HG_DK)).astype(BF16)
            o = att * v[:, hs] + jnp.dot(q8, s0.astype(BF16), preferred_element_type=F32)[0:1]
            sout_ref[b, h] = column(dec[:, hs]) * s0 + column(kh[:, hs]) * v[:, hs]
            o_ref[b, :, hs] = _rms(o) * gn_ref[...] * gate[:, hs]
        return carry

    lax.fori_loop(0, bb, body, 0)


def _hgrn_sample(gate3, lb, gn, state, layer, bb, stacked):
    m = gate3.shape[0]
    prev_specs = [] if stacked is None else [pl.BlockSpec(memory_space=pl.ANY)]
    prev_args = [] if stacked is None else [stacked]
    n_in = 7

    def seg(col0):
        blk = col0 // HG_WIDTH
        return pl.BlockSpec((bb, 1, HG_WIDTH), lambda i, blk=blk: (i, 0, blk))

    return pl.pallas_call(
        _hgrn_sample_kernel,
        grid=(m // bb,),
        in_specs=[seg(COL_QH), seg(COL_FH), seg(COL_IH), seg(COL_GH),
                  pl.BlockSpec((1, HG_WIDTH), lambda i: (0, 0)),
                  pl.BlockSpec((1, HG_DV), lambda i: (0, 0)),
                  pl.BlockSpec((None, bb, HG_HEADS, HG_DK, HG_DV),
                               lambda i: (layer, i, 0, 0, 0))] + prev_specs,
        out_specs=[pl.BlockSpec((bb, 1, HG_WIDTH), lambda i: (i, 0, 0)),
                   pl.BlockSpec((None, bb, HG_HEADS, HG_DK, HG_DV),
                                lambda i: (layer, i, 0, 0, 0))],
        out_shape=[jax.ShapeDtypeStruct((m, 1, HG_WIDTH), F32),
                   jax.ShapeDtypeStruct((DEPTH, m, HG_HEADS, HG_DK, HG_DV), F32)],
        input_output_aliases={} if stacked is None else {n_in: 1},
        compiler_params=_cparams(("parallel",)),
        name="hgrn_sample",
    )(gate3, gate3, gate3, gate3, lb, gn, state, *prev_args)


def _merge_kernel(x_ref, ya_ref, yb_ref, ga_ref, gb_ref, wpa_ref, wpb_ref, wo_ref, o_ref):
    tm = x_ref.shape[0]
    rc = min(tm, ROW_CHUNK)

    def branches(c):
        rs = slice(c * rc, (c + 1) * rc)
        return (jnp.dot(ya_ref[rs, :].astype(BF16), wpa_ref[...], preferred_element_type=F32),
                jnp.dot(yb_ref[rs, :].astype(BF16), wpb_ref[...], preferred_element_type=F32))

    nxt = branches(0)
    for c in range(tm // rc):
        rs = slice(c * rc, (c + 1) * rc)
        pa, pb = nxt
        if c + 1 < tm // rc:
            nxt = branches(c + 1)
        merged = jax.nn.sigmoid(ga_ref[rs, :]) * pa + jax.nn.sigmoid(gb_ref[rs, :]) * pb
        o_ref[rs, :] = x_ref[rs, :] + jnp.dot(merged.astype(BF16), wo_ref[...],
                                              preferred_element_type=F32)


def _merge(x, ya, yb, gate, wpa, wpb, wo, tm):
    m = x.shape[0]
    row = lambda w: pl.BlockSpec((tm, w), lambda i: (i, 0))
    full = lambda a: pl.BlockSpec(a.shape, lambda i: (0, 0))
    return pl.pallas_call(
        _merge_kernel,
        grid=(m // tm,),
        in_specs=[row(D_MODEL), row(GROUP_WIDTH), row(HG_WIDTH),
                  pl.BlockSpec((tm, D_MODEL), lambda i: (i, COL_GA // D_MODEL)),
                  pl.BlockSpec((tm, D_MODEL), lambda i: (i, COL_GB // D_MODEL)),
                  full(wpa), full(wpb), full(wo)],
        out_specs=row(D_MODEL),
        out_shape=jax.ShapeDtypeStruct((m, D_MODEL), F32),
        compiler_params=_cparams(("parallel",)),
        name="merge",
    )(x, ya, yb, gate, gate, wpa, wpb, wo)


def _mlp_kernel(x_ref, g2_ref, wup_ref, wdn_ref, o_ref, h_ref):
    f = pl.program_id(1)

    @pl.when(f == 0)
    def _():
        x = x_ref[...]
        h_ref[...] = (_rms(x) * g2_ref[...]).astype(BF16)
        o_ref[...] = x

    tm = x_ref.shape[0]
    rc = min(tm, ROW_CHUNK)

    def up(c):
        return jnp.dot(h_ref[c * rc:(c + 1) * rc, :], wup_ref[...], preferred_element_type=F32)

    nxt = up(0)
    for c in range(tm // rc):
        rs = slice(c * rc, (c + 1) * rc)
        u = nxt
        if c + 1 < tm // rc:
            nxt = up(c + 1)
        act = jnp.square(jnp.maximum(u, 0.0)).astype(BF16)
        o_ref[rs, :] += jnp.dot(act, wdn_ref[...], preferred_element_type=F32)


def _mlp(x, g2, wup, wdn, tm, tf):
    m = x.shape[0]
    return pl.pallas_call(
        _mlp_kernel,
        grid=(m // tm, D_FF // tf),
        in_specs=[pl.BlockSpec((tm, D_MODEL), lambda i, f: (i, 0)),
                  pl.BlockSpec((1, D_MODEL), lambda i, f: (0, 0)),
                  pl.BlockSpec((D_MODEL, tf), lambda i, f: (0, f)),
                  pl.BlockSpec((tf, D_MODEL), lambda i, f: (f, 0))],
        out_specs=pl.BlockSpec((tm, D_MODEL), lambda i, f: (i, 0)),
        out_shape=jax.ShapeDtypeStruct((m, D_MODEL), F32),
        scratch_shapes=[pltpu.VMEM((tm, D_MODEL), BF16)],
        compiler_params=_cparams(("parallel", "arbitrary")),
        name="mlp",
    )(x, g2, wup, wdn)


def _rope_tables(pos):
    half = HEAD_DIM // 2
    inv = ROPE_THETA ** (-(jnp.arange(half, dtype=F32) * 2.0 / HEAD_DIM))
    ang = pos.astype(F32)[:, None] * inv[None, :]
    cos, sin = jnp.cos(ang), jnp.sin(ang)
    return jnp.concatenate([cos, cos], axis=-1), jnp.concatenate([-sin, sin], axis=-1)


def kernel(x_prompt, x_sample, cache_k_w128, cache_v_w128, cache_k_w512, cache_v_w512,
           cache_k_w2048, cache_v_w2048, state_hgrn, w_in, norm1, q_norm, k_norm, hg_out_norm,
           hg_lower_bounds, w_pa, w_pb, w_o, norm2, w_up, w_down):
    bsz, seq, _ = x_prompt.shape
    dec_b, dec_t, _ = x_sample.shape
    mp = bsz * seq
    ms = dec_b * dec_t

    lb_all = jnp.cumsum(jax.nn.softmax(hg_lower_bounds.astype(F32), axis=0), axis=0)
    lb_all = lb_all - lb_all[0:1]

    cos_p, sin_p = _rope_tables(jnp.arange(seq))
    cos_s, sin_s = _rope_tables(jnp.broadcast_to(PAST_LEN + jnp.arange(dec_t), (ms,)))

    caches = []
    for (win, dil), (ck, cv) in zip(ATTN_GROUPS, ((cache_k_w128, cache_v_w128),
                                                   (cache_k_w512, cache_v_w512),
                                                   (cache_k_w2048, cache_v_w2048))):
        wc = ck.shape[2]
        shape = (DEPTH, dec_b, wc // dil, dil, ATTN_HEADS, HEAD_DIM)
        caches += [ck.reshape(shape), cv.reshape(shape)]

    w_in_b, w_pa_b, w_pb_b, w_o_b, w_up_b, w_dn_b = (
        w.astype(BF16) for w in (w_in, w_pa, w_pb, w_o, w_up, w_down))

    yp = x_prompt.reshape(mp, D_MODEL)
    ys = x_sample.reshape(ms, D_MODEL)
    keep_p = tuple(min(win, seq) for win, _ in ATTN_GROUPS)
    p_rows, s_rows = None, None
    p_states, s_states = [], None

    for layer in range(DEPTH):
        w_in_l = w_in_b[layer]
        wpa, wpb, wo = w_pa_b[layer], w_pb_b[layer], w_o_b[layer]
        wup, wdn = w_up_b[layer], w_dn_b[layer]
        g1 = norm1[layer].reshape(1, D_MODEL)
        g2 = norm2[layer].reshape(1, D_MODEL)
        gqk = jnp.concatenate([jnp.tile(q_norm[layer], (1, ATTN_HEADS)).reshape(1, QKV_WIDTH),
                               jnp.tile(k_norm[layer], (1, ATTN_HEADS)).reshape(1, QKV_WIDTH)],
                              axis=-1)
        gn = hg_out_norm[layer].reshape(1, HG_DV)
        lb = lb_all[layer].reshape(1, HG_WIDTH)

        gate = _in_proj_gate(yp, g1, w_in_l, tm=1024)
        qkv, p_rows = _in_proj_qkv(yp, g1, w_in_l, gqk, cos_p, sin_p, tm=1024, layer=layer,
                                   n_seq=bsz, keep=keep_p, stacked=p_rows)
        ya = _attn_prompt(qkv.reshape(bsz, seq, 3 * QKV_WIDTH))
        yb, st = _hgrn_prompt(gate.reshape(bsz, seq, GATE_WIDTH), lb, gn)
        yp = _merge(yp, ya.reshape(mp, GROUP_WIDTH), yb.reshape(mp, HG_WIDTH), gate,
                    wpa, wpb, wo, tm=512)
        yp = _mlp(yp, g2, wup, wdn, tm=512, tf=4096)
        p_states.append(st)

        gate_s = _in_proj_gate(ys, g1, w_in_l, tm=ms)
        qkv_s, s_rows = _in_proj_qkv(ys, g1, w_in_l, gqk, cos_s, sin_s, tm=ms, layer=layer,
                                     n_seq=1, keep=(ms,) * N_GROUPS, stacked=s_rows)
        ya_s = _attn_sample(qkv_s.reshape(ms, 1, 3 * QKV_WIDTH), caches, layer, bb=8)
        yb_s, s_states = _hgrn_sample(gate_s.reshape(ms, 1, GATE_WIDTH), lb, gn, state_hgrn,
                                      layer, bb=8, stacked=s_states)
        ys = _merge(ys, ya_s.reshape(ms, GROUP_WIDTH), yb_s.reshape(ms, HG_WIDTH), gate_s,
                    wpa, wpb, wo, tm=ms)
        ys = _mlp(ys, g2, wup, wdn, tm=ms, tf=1024)

    def interleave_kv(rows, n, keep):
        return [rows[c * N_GROUPS + g].reshape(DEPTH, n, keep[g], ATTN_HEADS, HEAD_DIM)
                for g in range(N_GROUPS) for c in range(2)]

    outs = [yp.reshape(bsz, seq, D_MODEL), ys.reshape(dec_b, dec_t, D_MODEL)]
    outs += interleave_kv(p_rows, bsz, keep_p)
    outs.append(jnp.stack(p_states, axis=0))
    outs += interleave_kv(s_rows, dec_b, (dec_t,) * N_GROUPS)
    outs.append(s_states)
    return tuple(outs)
```

```python
import functools

import jax
import jax.numpy as jnp
from jax import lax
from jax.experimental import pallas as pl
from jax.experimental.pallas import tpu as pltpu

F32 = jnp.float32
BF16 = jnp.bfloat16

D_MODEL = 1024
DEPTH = 4
PAST_LEN = 2048
ATTN_GROUPS = ((128, 1), (512, 4), (2048, 16))
N_GROUPS = 3
ATTN_HEADS = 4
HEAD_DIM = 128
GROUP_WIDTH = ATTN_HEADS * HEAD_DIM
QKV_WIDTH = N_GROUPS * GROUP_WIDTH
ATTN_NK = 128
ATTN_SCALE = HEAD_DIM ** -0.5
HG_HEADS = 8
HG_DK = 128
HG_DV = 128
HG_WIDTH = 1024
HG_CHUNK = 16
HG_BLOCK = 128
HG_HEAD_UNROLL = 4
D_FF = 4 * D_MODEL
ROPE_THETA = 10000.0
NORM_EPS = 1e-6
IN_WIDTH = 3 * QKV_WIDTH + 6 * HG_WIDTH
GATE_WIDTH = 6 * HG_WIDTH
NEG_BIG = -1e30

SEG_QH, SEG_FH, SEG_IH, SEG_GH, SEG_GA, SEG_GB = range(6)
ACT_Q, ACT_V, ACT_GH, ACT_GA, ACT_GB = range(5)
ACT_WIDTH = 5 * HG_WIDTH
COL_Q, COL_K, COL_V = 0, QKV_WIDTH, 2 * QKV_WIDTH

GATE_TILE = 1536
ROW_CHUNK = 256
ATTN_UNROLL = 8
SAMPLE_UNROLL = 2

VMEM_LIMIT = 48 * 1024 * 1024


def _cparams(sem):
    return pltpu.CompilerParams(dimension_semantics=sem, vmem_limit_bytes=VMEM_LIMIT)


def _rms(x):
    return x * lax.rsqrt(jnp.mean(x * x, axis=-1, keepdims=True) + NORM_EPS)


def _dot_nt(a, b):
    return lax.dot_general(a, b, (((1,), (1,)), ((), ())), preferred_element_type=F32)


def _in_proj_gate_kernel(x_ref, g1_ref, w_ref, lb_ref, act_ref, fg_ref, h_ref):
    j = pl.program_id(1)

    @pl.when(j == 0)
    def _():
        h_ref[...] = (_rms(x_ref[...]) * g1_ref[...]).astype(BF16)

    tm = x_ref.shape[0]
    rc = min(tm, ROW_CHUNK)
    piece = GROUP_WIDTH
    act_slot = {SEG_QH: ACT_Q, SEG_IH: ACT_V, SEG_GH: ACT_GH, SEG_GA: ACT_GA, SEG_GB: ACT_GB}

    def tile(jj):
        for c in range(tm // rc):
            rs = slice(c * rc, (c + 1) * rc)
            res = jnp.dot(h_ref[rs, :], w_ref[...], preferred_element_type=F32)
            for p in range(GATE_TILE // piece):
                col = jj * GATE_TILE + p * piece
                seg, off = col // HG_WIDTH, col % HG_WIDTH
                y = res[:, p * piece:(p + 1) * piece]
                if seg == SEG_FH:
                    lb = lb_ref[:, off:off + piece]
                    fg_ref[rs, off:off + piece] = lb + (1.0 - lb) * jax.nn.sigmoid(y)
                    continue
                if seg == SEG_QH:
                    y = jax.nn.silu(y)
                elif seg != SEG_IH:
                    y = jax.nn.sigmoid(y)
                dst = act_slot[seg] * HG_WIDTH + off
                act_ref[rs, dst:dst + piece] = y.astype(BF16)

    for jj in range(GATE_WIDTH // GATE_TILE):
        pl.when(j == jj)(functools.partial(tile, jj))


def _in_proj_gate(x, g1, w, lb, tm):
    m = x.shape[0]
    first = 3 * QKV_WIDTH // GATE_TILE
    return pl.pallas_call(
        _in_proj_gate_kernel,
        grid=(m // tm, GATE_WIDTH // GATE_TILE),
        in_specs=[
            pl.BlockSpec((tm, D_MODEL), lambda i, j: (i, 0)),
            pl.BlockSpec((1, D_MODEL), lambda i, j: (0, 0)),
            pl.BlockSpec((D_MODEL, GATE_TILE), lambda i, j: (0, first + j)),
            pl.BlockSpec((1, HG_WIDTH), lambda i, j: (0, 0)),
        ],
        out_specs=[pl.BlockSpec((tm, ACT_WIDTH), lambda i, j: (i, 0)),
                   pl.BlockSpec((tm, HG_WIDTH), lambda i, j: (i, 0))],
        out_shape=[jax.ShapeDtypeStruct((m, ACT_WIDTH), BF16),
                   jax.ShapeDtypeStruct((m, HG_WIDTH), F32)],
        scratch_shapes=[pltpu.VMEM((tm, D_MODEL), BF16)],
        compiler_params=_cparams(("parallel", "arbitrary")),
        name="in_proj_gate",
    )(x, g1, w, lb)


def _in_proj_qkv_kernel(keep_in_tile, x_ref, g1_ref, w_ref, gqk_ref, cos_ref, sin_ref, *rest):
    o_ref, h_ref = rest[-8], rest[-1]
    k_rows, v_rows = rest[-7:-4], rest[-4:-1]
    j = pl.program_id(1)

    @pl.when(j == 0)
    def _():
        h_ref[...] = (_rms(x_ref[...]) * g1_ref[...]).astype(BF16)

    tm = x_ref.shape[0]
    rc = min(tm, ROW_CHUNK)

    def tile(rope, row_refs):
        for c in range(tm // rc):
            rs = slice(c * rc, (c + 1) * rc)
            for g in range(N_GROUPS):
                res = jnp.dot(h_ref[rs, :], w_ref[:, g * GROUP_WIDTH:(g + 1) * GROUP_WIDTH],
                              preferred_element_type=F32)
                first_kept = tm - keep_in_tile[g]
                lo = max(c * rc, first_kept)
                for h in range(ATTN_HEADS):
                    cs = slice(g * GROUP_WIDTH + h * HEAD_DIM, g * GROUP_WIDTH + (h + 1) * HEAD_DIM)
                    y = res[:, h * HEAD_DIM:(h + 1) * HEAD_DIM]
                    if rope:
                        y = _rms(y) * gqk_ref[:, cs]
                        y = y * cos_ref[rs, :] + pltpu.roll(y, HEAD_DIM // 2, 1) * sin_ref[rs, :]
                    o_ref[rs, cs] = y
                    if row_refs is not None and lo < (c + 1) * rc:
                        n = (c + 1) * rc - lo
                        dst = (lo - first_kept) * ATTN_HEADS + h
                        row_refs[g][pl.ds(dst, n, stride=ATTN_HEADS), :] = y[lo - c * rc:, :]

    pl.when(j == 0)(lambda: tile(True, None))
    pl.when(j == 1)(lambda: tile(True, k_rows))
    pl.when(j == 2)(lambda: tile(False, v_rows))


def _in_proj_qkv(x, g1, w, gqk, cos, sin, tm, layer, n_seq, keep, stacked):
    m = x.shape[0]
    n_pos_blocks = cos.shape[0] // tm
    tiles_per_seq = m // n_seq // tm
    keep_in_tile = tuple(min(k, tm) for k in keep)
    row_specs, row_shapes = [], []
    for k, kt in zip(keep, keep_in_tile):
        first_tile = tiles_per_seq - k // kt

        def index(i, j, first_tile=first_tile):
            return (layer, i // tiles_per_seq, jnp.maximum(i % tiles_per_seq - first_tile, 0), 0)

        row_specs.append(pl.BlockSpec((None, None, kt * ATTN_HEADS, HEAD_DIM), index))
        row_shapes.append(jax.ShapeDtypeStruct((DEPTH, n_seq, k * ATTN_HEADS, HEAD_DIM), F32))
    n_in = 6
    prev_specs = [] if stacked is None else [pl.BlockSpec(memory_space=pl.ANY)] * 6
    prev_args = [] if stacked is None else list(stacked)
    aliases = {} if stacked is None else {n_in + r: 1 + r for r in range(6)}
    outs = pl.pallas_call(
        functools.partial(_in_proj_qkv_kernel, keep_in_tile),
        grid=(m // tm, 3),
        in_specs=[
            pl.BlockSpec((tm, D_MODEL), lambda i, j: (i, 0)),
            pl.BlockSpec((1, D_MODEL), lambda i, j: (0, 0)),
            pl.BlockSpec((D_MODEL, QKV_WIDTH), lambda i, j: (0, j)),
            pl.BlockSpec((1, QKV_WIDTH), lambda i, j: (0, jnp.minimum(j, 1))),
            pl.BlockSpec((tm, HEAD_DIM), lambda i, j: (i % n_pos_blocks, 0)),
            pl.BlockSpec((tm, HEAD_DIM), lambda i, j: (i % n_pos_blocks, 0)),
        ] + prev_specs,
        out_specs=[pl.BlockSpec((tm, QKV_WIDTH), lambda i, j: (i, j))] + row_specs * 2,
        out_shape=[jax.ShapeDtypeStruct((m, 3 * QKV_WIDTH), F32)] + row_shapes * 2,
        input_output_aliases=aliases,
        scratch_shapes=[pltpu.VMEM((tm, D_MODEL), BF16)],
        compiler_params=_cparams(("arbitrary", "arbitrary")),
        name="in_proj_qkv",
    )(x, g1, w, gqk, cos, sin, *prev_args)
    return outs[0], outs[1:]


def _attn_prompt_kernel(q0, q1, q2, k0, k1, k2, v0, v1, v2, o_ref, m_sc, l_sc, a_sc):
    seq = o_ref.shape[0]
    nk = ATTN_NK
    row = lax.broadcasted_iota(jnp.int32, (nk, nk), 0)
    col = lax.broadcasted_iota(jnp.int32, (nk, nk), 1)
    mask_cur = col <= row
    mask_prev = col >= row

    def rows(start, stride):
        if stride == 1:
            return pl.ds(start, nk)
        return pl.ds(start, nk, stride=stride)

    def load(qkv, dil, starts):
        q_ref, k_ref, v_ref = qkv
        loaded = []
        for start, prev_start in starts:
            sl = rows(start, dil)
            item = [q_ref[sl, :], k_ref[sl, :], v_ref[sl, :]]
            if prev_start is not None:
                slp = rows(prev_start, dil)
                item += [k_ref[slp, :], v_ref[slp, :]]
            loaded.append(item)
        return loaded

    def attend(loaded):
        scored = []
        for item in loaded:
            q = item[0].astype(BF16)
            s = [jnp.where(mask_cur, _dot_nt(q, item[1].astype(BF16)) * ATTN_SCALE, NEG_BIG)]
            if len(item) > 3:
                s.append(jnp.where(mask_prev, _dot_nt(q, item[3].astype(BF16)) * ATTN_SCALE,
                                   NEG_BIG))
            scored.append(s)
        probs = []
        for s in scored:
            m = jnp.max(s[0] if len(s) == 1 else jnp.maximum(s[0], s[1]), axis=-1, keepdims=True)
            p = [jnp.exp(x - m) for x in s]
            l = jnp.sum(p[0] if len(s) == 1 else p[0] + p[1], axis=-1, keepdims=True)
            probs.append((m, l, [x.astype(BF16) for x in p]))
        results = []
        for item, (m, l, p) in zip(loaded, probs):
            acc = jnp.dot(p[0], item[2].astype(BF16), preferred_element_type=F32)
            if len(p) > 1:
                acc = acc + jnp.dot(p[1], item[4].astype(BF16), preferred_element_type=F32)
            results.append((m, l, acc))
        return results

    def keep(g, dil, starts, results):
        for (start, _), (m, l, acc) in zip(starts, results):
            sl = rows(start, dil)
            m_sc[g, sl, :] = jnp.broadcast_to(m, (nk, HEAD_DIM))
            l_sc[g, sl, :] = jnp.broadcast_to(l, (nk, HEAD_DIM))
            a_sc[g, sl, :] = acc

    def blocks(g, qkv, dil, starts):
        keep(g, dil, starts, attend(load(qkv, dil, starts)))

    refs = ((q0, k0, v0), (q1, k1, v1), (q2, k2, v2))
    unr = ATTN_UNROLL
    for g, (_, dil) in enumerate(ATTN_GROUPS):
        qkv = refs[g]
        nb = seq // dil // nk
        stride = nk * dil
        if nb == 1:
            def body(it, carry, g=g, qkv=qkv, dil=dil):
                blocks(g, qkv, dil, [(it * unr + u, None) for u in range(unr)])
                return carry
            lax.fori_loop(0, dil // unr, body, 0)
        elif dil == 1:
            blocks(g, qkv, dil, [(n * nk, None if n == 0 else (n - 1) * nk) for n in range(unr)])

            def body(it, carry, g=g, qkv=qkv, dil=dil):
                starts = [pl.multiple_of((it * unr + u) * nk, nk) for u in range(unr)]
                blocks(g, qkv, dil, [(s, s - nk) for s in starts])
                return carry
            lax.fori_loop(1, nb // unr, body, 0)
        else:
            blocks(g, qkv, dil, [(r, None) for r in range(dil)])

            def body(n, carry, g=g, qkv=qkv, dil=dil, stride=stride):
                blocks(g, qkv, dil, [(r + n * stride, r + (n - 1) * stride) for r in range(dil)])
                return carry
            lax.fori_loop(1, nb, body, 0)

    m_all = jnp.maximum(jnp.maximum(m_sc[0], m_sc[1]), m_sc[2])
    num = jnp.zeros((seq, HEAD_DIM), F32)
    den = jnp.zeros((seq, HEAD_DIM), F32)
    for g in range(N_GROUPS):
        w = jnp.exp(m_sc[g] - m_all)
        num = num + w * a_sc[g]
        den = den + w * l_sc[g]
    o_ref[...] = (num / den).astype(o_ref.dtype)


def _attn_prompt(qkv3):
    b, seq, _ = qkv3.shape

    def spec(col0, g):
        base = (col0 + g * GROUP_WIDTH) // HEAD_DIM
        return pl.BlockSpec((None, seq, HEAD_DIM), lambda bi, h, base=base: (bi, 0, base + h))

    in_specs = ([spec(COL_Q, g) for g in range(N_GROUPS)]
                + [spec(COL_K, g) for g in range(N_GROUPS)]
                + [spec(COL_V, g) for g in range(N_GROUPS)])
    return pl.pallas_call(
        _attn_prompt_kernel,
        grid=(b, ATTN_HEADS),
        in_specs=in_specs,
        out_specs=pl.BlockSpec((None, seq, HEAD_DIM), lambda bi, h: (bi, 0, h)),
        out_shape=jax.ShapeDtypeStruct((b, seq, GROUP_WIDTH), BF16),
        scratch_shapes=[pltpu.VMEM((N_GROUPS, seq, HEAD_DIM), F32)] * 3,
        compiler_params=_cparams(("parallel", "parallel")),
        name="attn_prompt",
    )(*([qkv3] * 9))


def _attn_sample_kernel(qkv_ref, kc0, vc0, kc1, vc1, kc2, vc2, o_ref):
    bb = o_ref.shape[0]
    caches = ((kc0, vc0), (kc1, vc1), (kc2, vc2))
    n_rows = ATTN_NK * ATTN_HEADS
    r8 = lax.broadcasted_iota(jnp.int32, (8, n_rows), 0)
    c8 = lax.broadcasted_iota(jnp.int32, (8, n_rows), 1)
    own = (r8 & (ATTN_HEADS - 1)) == (c8 & (ATTN_HEADS - 1))

    def window(cache_ref, b):
        return cache_ref[b].reshape(n_rows, HEAD_DIM).astype(BF16)

    def heads_on_rows(row, col0):
        hs = [row[:, col0 + h * HEAD_DIM:col0 + (h + 1) * HEAD_DIM] for h in range(ATTN_HEADS)]
        return jnp.concatenate(hs + hs, axis=0)

    def body(it, carry):
        items = []
        for u in range(SAMPLE_UNROLL):
            b = it * SAMPLE_UNROLL + u
            row = qkv_ref[b]
            for g in range(N_GROUPS):
                lo = g * GROUP_WIDTH
                q = heads_on_rows(row, COL_Q + lo)
                kn = heads_on_rows(row, COL_K + lo)
                vn = heads_on_rows(row, COL_V + lo)
                s = _dot_nt(q.astype(BF16), window(caches[g][0], b)) * ATTN_SCALE
                s = jnp.where(own, s, NEG_BIG)
                sn = jnp.sum(q * kn, axis=-1, keepdims=True) * ATTN_SCALE
                items.append((b, g, s, sn, vn))
        probs = []
        for b, g, s, sn, vn in items:
            m = jnp.maximum(jnp.max(s, axis=-1, keepdims=True), sn)
            p = jnp.exp(s - m)
            pn = jnp.exp(sn - m)
            probs.append((m, jnp.sum(p, axis=-1, keepdims=True) + pn, p.astype(BF16), pn))
        accs = [jnp.dot(p, window(caches[g][1], b), preferred_element_type=F32) + pn * vn
                for (b, g, _, _, vn), (_, _, p, pn) in zip(items, probs)]
        for u in range(SAMPLE_UNROLL):
            sl = slice(u * N_GROUPS, (u + 1) * N_GROUPS)
            ms = [pr[0] for pr in probs[sl]]
            ls = [pr[1] for pr in probs[sl]]
            m_all = jnp.maximum(jnp.maximum(ms[0], ms[1]), ms[2])
            num = jnp.zeros((8, HEAD_DIM), F32)
            den = jnp.zeros((8, 1), F32)
            for g in range(N_GROUPS):
                w = jnp.exp(ms[g] - m_all)
                num = num + w * accs[sl][g]
                den = den + w * ls[g]
            out = num / den
            o_ref[it * SAMPLE_UNROLL + u] = jnp.concatenate(
                [out[h:h + 1] for h in range(ATTN_HEADS)], axis=1)
        return carry

    lax.fori_loop(0, bb // SAMPLE_UNROLL, body, 0)


def _attn_sample(qkv3, caches, layer, bb):
    m = qkv3.shape[0]
    cache_specs = [pl.BlockSpec((None, bb, ATTN_NK, None, ATTN_HEADS, HEAD_DIM),
                                lambda i: (layer, i, 0, 0, 0, 0))
                   for _ in range(2 * N_GROUPS)]
    return pl.pallas_call(
        _attn_sample_kernel,
        grid=(m // bb,),
        in_specs=[pl.BlockSpec((bb, 1, 3 * QKV_WIDTH), lambda i: (i, 0, 0))] + cache_specs,
        out_specs=pl.BlockSpec((bb, 1, GROUP_WIDTH), lambda i: (i, 0, 0)),
        out_shape=jax.ShapeDtypeStruct((m, 1, GROUP_WIDTH), F32),
        compiler_params=_cparams(("parallel",)),
        name="attn_sample",
    )(qkv3, *caches)


def _cumsum8(y, rid):
    for s in (1, 2, 4):
        y = y + jnp.where(rid >= s, pltpu.roll(y, s, 0), 0.0)
    return y


def _hgrn_prompt_kernel(q_ref, fg_ref, v_ref, gate_ref, gn_ref, o_ref, sfin_ref, st_sc):
    t = pl.program_id(1)
    blk = HG_BLOCK
    c = HG_CHUNK
    nch = blk // c

    @pl.when(t == 0)
    def _():
        st_sc[...] = jnp.zeros_like(st_sc)

    rowi = lax.broadcasted_iota(jnp.int32, (blk, blk), 0)
    coli = lax.broadcasted_iota(jnp.int32, (blk, blk), 1)
    causal = coli <= rowi
    rid8 = lax.broadcasted_iota(jnp.int32, (8, HG_DK), 0)

    def head_scores(q, fg):
        q = q.astype(F32)
        lf = jnp.log(fg)
        kh = 1.0 - fg

        slabs = []
        for k in range(blk // 8):
            s8 = _cumsum8(lf[8 * k:8 * k + 8], rid8)
            if k % 2 == 1:
                s8 = s8 + slabs[k - 1][7:8, :]
            slabs.append(s8)
        g_pre = jnp.zeros((1, HG_DK), F32)
        g_starts, acum_slabs = [], []
        for i in range(nch):
            g_starts.append(g_pre)
            acum_slabs += [slabs[2 * i] + g_pre, slabs[2 * i + 1] + g_pre]
            g_pre = g_pre + slabs[2 * i + 1][7:8, :]
        g_end = g_pre
        a = jnp.concatenate(slabs, axis=0)
        acum = jnp.concatenate(acum_slabs, axis=0)

        qt = (q * jnp.exp(a)).astype(BF16)
        qhat = (q * jnp.exp(acum)).astype(BF16)
        khat = (kh * jnp.exp(g_end - acum)).astype(BF16)

        prow = []
        for i in range(nch):
            n = c * (i + 1)
            kt = (kh[:n] * jnp.exp(g_starts[i] - acum[:n])).astype(BF16)
            if n < blk:
                kt = jnp.concatenate([kt, jnp.zeros((blk - n, HG_DK), BF16)], axis=0)
            prow.append(_dot_nt(qt[c * i:c * (i + 1)], kt))
        return prow, qhat, khat, jnp.exp(g_end)

    def head_output(scores, v, gate, st0):
        prow, qhat, khat, dec = scores
        p = jnp.where(causal, jnp.concatenate(prow, axis=0), 0.0).astype(BF16)
        vt = v.astype(F32).T.astype(BF16)
        o = _dot_nt(jnp.concatenate([p, qhat], axis=1),
                    jnp.concatenate([vt, st0.astype(BF16)], axis=1))
        o = _rms(o) * gn_ref[...] * gate.astype(F32)
        return o, st0 * dec + jnp.dot(vt, khat, preferred_element_type=F32)

    def head_group(i, carry):
        heads = [i * HG_HEAD_UNROLL + u for u in range(HG_HEAD_UNROLL)]
        cols = [pl.ds(pl.multiple_of(h * HG_DK, HG_DK), HG_DK) for h in heads]
        loaded = [(q_ref[:, cs], fg_ref[:, cs], v_ref[:, cs], gate_ref[:, cs], st_sc[h])
                  for h, cs in zip(heads, cols)]
        scores = [head_scores(*args[:2]) for args in loaded]
        results = [head_output(sc, *args[2:]) for sc, args in zip(scores, loaded)]
        for h, cs, (o, st_new) in zip(heads, cols, results):
            o_ref[:, cs] = o.astype(o_ref.dtype)
            st_sc[h] = st_new
        return carry

    lax.fori_loop(0, HG_HEADS // HG_HEAD_UNROLL, head_group, 0)

    @pl.when(t == pl.num_programs(1) - 1)
    def _():
        for h in range(HG_HEADS):
            sfin_ref[h] = st_sc[h].T


def _hgrn_prompt(act3, fg3, gn):
    b, seq, _ = act3.shape

    def seg(blk):
        return pl.BlockSpec((None, HG_BLOCK, HG_WIDTH), lambda bi, t, blk=blk: (bi, t, blk))

    return pl.pallas_call(
        _hgrn_prompt_kernel,
        grid=(b, seq // HG_BLOCK),
        in_specs=[seg(ACT_Q), seg(0), seg(ACT_V), seg(ACT_GH),
                  pl.BlockSpec((1, HG_DV), lambda bi, t: (0, 0))],
        out_specs=[pl.BlockSpec((None, HG_BLOCK, HG_WIDTH), lambda bi, t: (bi, t, 0)),
                   pl.BlockSpec((None, HG_HEADS, HG_DK, HG_DV), lambda bi, t: (bi, 0, 0, 0))],
        out_shape=[jax.ShapeDtypeStruct((b, seq, HG_WIDTH), BF16),
                   jax.ShapeDtypeStruct((b, HG_HEADS, HG_DK, HG_DV), F32)],
        scratch_shapes=[pltpu.VMEM((HG_HEADS, HG_DV, HG_DK), F32)],
        compiler_params=_cparams(("parallel", "arbitrary")),
        name="hgrn_prompt",
    )(act3, fg3, act3, act3, gn)


def _hgrn_sample_kernel(q_ref, fg_ref, v_ref, gate_ref, gn_ref, s_ref, *rest):
    o_ref, sout_ref = rest[-2:]
    bb = q_ref.shape[0]
    ri = lax.broadcasted_iota(jnp.int32, (HG_DK, HG_DK), 0)
    ci = lax.broadcasted_iota(jnp.int32, (HG_DK, HG_DK), 1)
    diag = ri == ci

    def column(rowvec):
        full = jnp.broadcast_to(rowvec, (HG_DK, HG_DK))
        return jnp.sum(jnp.where(diag, full, 0.0), axis=1, keepdims=True)

    def body(b, carry):
        q = q_ref[b].astype(F32)
        fg = fg_ref[b]
        v = v_ref[b].astype(F32)
        gate = gate_ref[b].astype(F32)
        a = jnp.log(fg)
        kh = 1.0 - fg
        qt = q * jnp.exp(a)
        kt = kh * jnp.exp(-a)
        dec = jnp.exp(a)
        for h in range(HG_HEADS):
            hs = slice(h * HG_DK, (h + 1) * HG_DK)
            s0 = s_ref[b, h]
            att = jnp.sum(qt[:, hs] * kt[:, hs], axis=-1, keepdims=True)
            q8 = jnp.broadcast_to(qt[:, hs], (8, HG_DK)).astype(BF16)
            o = att * v[:, hs] + jnp.dot(q8, s0.astype(BF16), preferred_element_type=F32)[0:1]
            sout_ref[b, h] = column(dec[:, hs]) * s0 + column(kh[:, hs]) * v[:, hs]
            o_ref[b, :, hs] = _rms(o) * gn_ref[...] * gate[:, hs]
        return carry

    lax.fori_loop(0, bb, body, 0)


def _hgrn_sample(act3, fg3, gn, state, layer, bb, stacked):
    m = act3.shape[0]
    prev_specs = [] if stacked is None else [pl.BlockSpec(memory_space=pl.ANY)]
    prev_args = [] if stacked is None else [stacked]
    n_in = 6

    def seg(blk):
        return pl.BlockSpec((bb, 1, HG_WIDTH), lambda i, blk=blk: (i, 0, blk))

    return pl.pallas_call(
        _hgrn_sample_kernel,
        grid=(m // bb,),
        in_specs=[seg(ACT_Q), seg(0), seg(ACT_V), seg(ACT_GH),
                  pl.BlockSpec((1, HG_DV), lambda i: (0, 0)),
                  pl.BlockSpec((None, bb, HG_HEADS, HG_DK, HG_DV),
                               lambda i: (layer, i, 0, 0, 0))] + prev_specs,
        out_specs=[pl.BlockSpec((bb, 1, HG_WIDTH), lambda i: (i, 0, 0)),
                   pl.BlockSpec((None, bb, HG_HEADS, HG_DK, HG_DV),
                                lambda i: (layer, i, 0, 0, 0))],
        out_shape=[jax.ShapeDtypeStruct((m, 1, HG_WIDTH), F32),
                   jax.ShapeDtypeStruct((DEPTH, m, HG_HEADS, HG_DK, HG_DV), F32)],
        input_output_aliases={} if stacked is None else {n_in: 1},
        compiler_params=_cparams(("parallel",)),
        name="hgrn_sample",
    )(act3, fg3, act3, act3, gn, state, *prev_args)


def _merge_kernel(x_ref, ya_ref, yb_ref, ga_ref, gb_ref, wpa_ref, wpb_ref, wo_ref, o_ref):
    tm = x_ref.shape[0]
    rc = min(tm, ROW_CHUNK)

    def branches(c):
        rs = slice(c * rc, (c + 1) * rc)
        return (jnp.dot(ya_ref[rs, :].astype(BF16), wpa_ref[...], preferred_element_type=F32),
                jnp.dot(yb_ref[rs, :].astype(BF16), wpb_ref[...], preferred_element_type=F32))

    nxt = branches(0)
    for c in range(tm // rc):
        rs = slice(c * rc, (c + 1) * rc)
        pa, pb = nxt
        if c + 1 < tm // rc:
            nxt = branches(c + 1)
        merged = ga_ref[rs, :].astype(F32) * pa + gb_ref[rs, :].astype(F32) * pb
        o_ref[rs, :] = x_ref[rs, :] + jnp.dot(merged.astype(BF16), wo_ref[...],
                                              preferred_element_type=F32)


def _merge(x, ya, yb, act, wpa, wpb, wo, tm):
    m = x.shape[0]
    row = lambda w: pl.BlockSpec((tm, w), lambda i: (i, 0))
    full = lambda a: pl.BlockSpec(a.shape, lambda i: (0, 0))
    return pl.pallas_call(
        _merge_kernel,
        grid=(m // tm,),
        in_specs=[row(D_MODEL), row(GROUP_WIDTH), row(HG_WIDTH),
                  pl.BlockSpec((tm, HG_WIDTH), lambda i: (i, ACT_GA)),
                  pl.BlockSpec((tm, HG_WIDTH), lambda i: (i, ACT_GB)),
                  full(wpa), full(wpb), full(wo)],
        out_specs=row(D_MODEL),
        out_shape=jax.ShapeDtypeStruct((m, D_MODEL), F32),
        compiler_params=_cparams(("parallel",)),
        name="merge",
    )(x, ya, yb, act, act, wpa, wpb, wo)


def _mlp_kernel(x_ref, g2_ref, wup_ref, wdn_ref, o_ref, h_ref):
    f = pl.program_id(1)

    @pl.when(f == 0)
    def _():
        x = x_ref[...]
        h_ref[...] = (_rms(x) * g2_ref[...]).astype(BF16)
        o_ref[...] = x

    tm = x_ref.shape[0]
    rc = min(tm, ROW_CHUNK)

    def up(c):
        return jnp.dot(h_ref[c * rc:(c + 1) * rc, :], wup_ref[...], preferred_element_type=F32)

    nxt = up(0)
    for c in range(tm // rc):
        rs = slice(c * rc, (c + 1) * rc)
        u = nxt
        if c + 1 < tm // rc:
            nxt = up(c + 1)
        act = jnp.square(jnp.maximum(u, 0.0)).astype(BF16)
        o_ref[rs, :] += jnp.dot(act, wdn_ref[...], preferred_element_type=F32)


def _mlp(x, g2, wup, wdn, tm, tf):
    m = x.shape[0]
    return pl.pallas_call(
        _mlp_kernel,
        grid=(m // tm, D_FF // tf),
        in_specs=[pl.BlockSpec((tm, D_MODEL), lambda i, f: (i, 0)),
                  pl.BlockSpec((1, D_MODEL), lambda i, f: (0, 0)),
                  pl.BlockSpec((D_MODEL, tf), lambda i, f: (0, f)),
                  pl.BlockSpec((tf, D_MODEL), lambda i, f: (f, 0))],
        out_specs=pl.BlockSpec((tm, D_MODEL), lambda i, f: (i, 0)),
        out_shape=jax.ShapeDtypeStruct((m, D_MODEL), F32),
        scratch_shapes=[pltpu.VMEM((tm, D_MODEL), BF16)],
        compiler_params=_cparams(("parallel", "arbitrary")),
        name="mlp",
    )(x, g2, wup, wdn)


def _rope_tables(pos):
    half = HEAD_DIM // 2
    inv = ROPE_THETA ** (-(jnp.arange(half, dtype=F32) * 2.0 / HEAD_DIM))
    ang = pos.astype(F32)[:, None] * inv[None, :]
    cos, sin = jnp.cos(ang), jnp.sin(ang)
    return jnp.concatenate([cos, cos], axis=-1), jnp.concatenate([-sin, sin], axis=-1)


def kernel(x_prompt, x_sample, cache_k_w128, cache_v_w128, cache_k_w512, cache_v_w512,
           cache_k_w2048, cache_v_w2048, state_hgrn, w_in, norm1, q_norm, k_norm, hg_out_norm,
           hg_lower_bounds, w_pa, w_pb, w_o, norm2, w_up, w_down):
    bsz, seq, _ = x_prompt.shape
    dec_b, dec_t, _ = x_sample.shape
    mp = bsz * seq
    ms = dec_b * dec_t

    lb_all = jnp.cumsum(jax.nn.softmax(hg_lower_bounds.astype(F32), axis=0), axis=0)
    lb_all = lb_all - lb_all[0:1]

    cos_p, sin_p = _rope_tables(jnp.arange(seq))
    cos_s, sin_s = _rope_tables(jnp.broadcast_to(PAST_LEN + jnp.arange(dec_t), (ms,)))

    caches = []
    for (win, dil), (ck, cv) in zip(ATTN_GROUPS, ((cache_k_w128, cache_v_w128),
                                                   (cache_k_w512, cache_v_w512),
                                                   (cache_k_w2048, cache_v_w2048))):
        wc = ck.shape[2]
        shape = (DEPTH, dec_b, wc // dil, dil, ATTN_HEADS, HEAD_DIM)
        caches += [ck.reshape(shape), cv.reshape(shape)]

    w_in_b, w_pa_b, w_pb_b, w_o_b, w_up_b, w_dn_b = (
        w.astype(BF16) for w in (w_in, w_pa, w_pb, w_o, w_up, w_down))

    yp = x_prompt.reshape(mp, D_MODEL)
    ys = x_sample.reshape(ms, D_MODEL)
    keep_p = tuple(min(win, seq) for win, _ in ATTN_GROUPS)
    p_rows, s_rows = None, None
    p_states, s_states = [], None

    for layer in range(DEPTH):
        w_in_l = w_in_b[layer]
        wpa, wpb, wo = w_pa_b[layer], w_pb_b[layer], w_o_b[layer]
        wup, wdn = w_up_b[layer], w_dn_b[layer]
        g1 = norm1[layer].reshape(1, D_MODEL)
        g2 = norm2[layer].reshape(1, D_MODEL)
        gqk = jnp.concatenate([jnp.tile(q_norm[layer], (1, ATTN_HEADS)).reshape(1, QKV_WIDTH),
                               jnp.tile(k_norm[layer], (1, ATTN_HEADS)).reshape(1, QKV_WIDTH)],
                              axis=-1)
        gn = hg_out_norm[layer].reshape(1, HG_DV)
        lb = lb_all[layer].reshape(1, HG_WIDTH)

        act, fg = _in_proj_gate(yp, g1, w_in_l, lb, tm=1024)
        qkv, p_rows = _in_proj_qkv(yp, g1, w_in_l, gqk, cos_p, sin_p, tm=1024, layer=layer,
                                   n_seq=bsz, keep=keep_p, stacked=p_rows)
        ya = _attn_prompt(qkv.reshape(bsz, seq, 3 * QKV_WIDTH))
        yb, st = _hgrn_prompt(act.reshape(bsz, seq, ACT_WIDTH), fg.reshape(bsz, seq, HG_WIDTH), gn)
        yp = _merge(yp, ya.reshape(mp, GROUP_WIDTH), yb.reshape(mp, HG_WIDTH), act,
                    wpa, wpb, wo, tm=512)
        yp = _mlp(yp, g2, wup, wdn, tm=512, tf=4096)
        p_states.append(st)

        act_s, fg_s = _in_proj_gate(ys, g1, w_in_l, lb, tm=ms)
        qkv_s, s_rows = _in_proj_qkv(ys, g1, w_in_l, gqk, cos_s, sin_s, tm=ms, layer=layer,
                                     n_seq=1, keep=(ms,) * N_GROUPS, stacked=s_rows)
        ya_s = _attn_sample(qkv_s.reshape(ms, 1, 3 * QKV_WIDTH), caches, layer, bb=8)
        yb_s, s_states = _hgrn_sample(act_s.reshape(ms, 1, ACT_WIDTH), fg_s.reshape(ms, 1, HG_WIDTH),
                                      gn, state_hgrn, layer, bb=8, stacked=s_states)
        ys = _merge(ys, ya_s.reshape(ms, GROUP_WIDTH), yb_s.reshape(ms, HG_WIDTH), act_s,
                    wpa, wpb, wo, tm=ms)
        ys = _mlp(ys, g2, wup, wdn, tm=ms, tf=1024)

    def interleave_kv(rows, n, keep):
        return [rows[c * N_GROUPS + g].reshape(DEPTH, n, keep[g], ATTN_HEADS, HEAD_DIM)
                for g in range(N_GROUPS) for c in range(2)]

    outs = [yp.reshape(bsz, seq, D_MODEL), ys.reshape(dec_b, dec_t, D_MODEL)]
    outs += interleave_kv(p_rows, bsz, keep_p)
    outs.append(jnp.stack(p_states, axis=0))
    outs += interleave_kv(s_rows, dec_b, (dec_t,) * N_GROUPS)
    outs.append(s_states)
    return tuple(outs)
```

```python
import functools

import jax
import jax.numpy as jnp
from jax import lax
from jax.experimental import pallas as pl
from jax.experimental.pallas import tpu as pltpu

F32 = jnp.float32
BF16 = jnp.bfloat16

D_MODEL = 1024
DEPTH = 4
PAST_LEN = 2048
ATTN_GROUPS = ((128, 1), (512, 4), (2048, 16))
N_GROUPS = 3
ATTN_HEADS = 4
HEAD_DIM = 128
GROUP_WIDTH = ATTN_HEADS * HEAD_DIM
QKV_WIDTH = N_GROUPS * GROUP_WIDTH
ATTN_NK = 128
ATTN_SCALE = HEAD_DIM ** -0.5
HG_HEADS = 8
HG_DK = 128
HG_DV = 128
HG_WIDTH = 1024
HG_CHUNK = 16
HG_BLOCK = 128
HG_HEAD_UNROLL = 4
D_FF = 4 * D_MODEL
ROPE_THETA = 10000.0
NORM_EPS = 1e-6
IN_WIDTH = 3 * QKV_WIDTH + 6 * HG_WIDTH
GATE_WIDTH = 6 * HG_WIDTH
NEG_BIG = -1e30

SEG_QH, SEG_FH, SEG_IH, SEG_GH, SEG_GA, SEG_GB = range(6)
ACT_Q, ACT_V, ACT_GH, ACT_GA, ACT_GB = range(5)
ACT_WIDTH = 5 * HG_WIDTH
COL_Q, COL_K, COL_V = 0, QKV_WIDTH, 2 * QKV_WIDTH

GATE_TILE = 1536
ROW_CHUNK = 256
ATTN_UNROLL = 8
SAMPLE_UNROLL = 2

VMEM_LIMIT = 48 * 1024 * 1024
POST_VMEM_LIMIT = 56 * 1024 * 1024


def _cparams(sem):
    return pltpu.CompilerParams(dimension_semantics=sem, vmem_limit_bytes=VMEM_LIMIT)


def _rms(x):
    return x * lax.rsqrt(jnp.mean(x * x, axis=-1, keepdims=True) + NORM_EPS)


def _dot_nt(a, b):
    return lax.dot_general(a, b, (((1,), (1,)), ((), ())), preferred_element_type=F32)


def _in_proj_gate_kernel(x_ref, g1_ref, w_ref, lb_ref, act_ref, fg_ref, h_ref):
    j = pl.program_id(1)

    @pl.when(j == 0)
    def _():
        h_ref[...] = (_rms(x_ref[...]) * g1_ref[...]).astype(BF16)

    tm = x_ref.shape[0]
    rc = min(tm, ROW_CHUNK)
    piece = GROUP_WIDTH
    act_slot = {SEG_QH: ACT_Q, SEG_IH: ACT_V, SEG_GH: ACT_GH, SEG_GA: ACT_GA, SEG_GB: ACT_GB}

    def tile(jj):
        for c in range(tm // rc):
            rs = slice(c * rc, (c + 1) * rc)
            res = jnp.dot(h_ref[rs, :], w_ref[...], preferred_element_type=F32)
            for p in range(GATE_TILE // piece):
                col = jj * GATE_TILE + p * piece
                seg, off = col // HG_WIDTH, col % HG_WIDTH
                y = res[:, p * piece:(p + 1) * piece]
                if seg == SEG_FH:
                    lb = lb_ref[:, off:off + piece]
                    fg_ref[rs, off:off + piece] = lb + (1.0 - lb) * jax.nn.sigmoid(y)
                    continue
                if seg == SEG_QH:
                    y = jax.nn.silu(y)
                elif seg != SEG_IH:
                    y = jax.nn.sigmoid(y)
                dst = act_slot[seg] * HG_WIDTH + off
                act_ref[rs, dst:dst + piece] = y.astype(BF16)

    for jj in range(GATE_WIDTH // GATE_TILE):
        pl.when(j == jj)(functools.partial(tile, jj))


def _in_proj_gate(x, g1, w, lb, tm):
    m = x.shape[0]
    first = 3 * QKV_WIDTH // GATE_TILE
    return pl.pallas_call(
        _in_proj_gate_kernel,
        grid=(m // tm, GATE_WIDTH // GATE_TILE),
        in_specs=[
            pl.BlockSpec((tm, D_MODEL), lambda i, j: (i, 0)),
            pl.BlockSpec((1, D_MODEL), lambda i, j: (0, 0)),
            pl.BlockSpec((D_MODEL, GATE_TILE), lambda i, j: (0, first + j)),
            pl.BlockSpec((1, HG_WIDTH), lambda i, j: (0, 0)),
        ],
        out_specs=[pl.BlockSpec((tm, ACT_WIDTH), lambda i, j: (i, 0)),
                   pl.BlockSpec((tm, HG_WIDTH), lambda i, j: (i, 0))],
        out_shape=[jax.ShapeDtypeStruct((m, ACT_WIDTH), BF16),
                   jax.ShapeDtypeStruct((m, HG_WIDTH), F32)],
        scratch_shapes=[pltpu.VMEM((tm, D_MODEL), BF16)],
        compiler_params=_cparams(("parallel", "arbitrary")),
        name="in_proj_gate",
    )(x, g1, w, lb)


def _in_proj_qkv_kernel(keep_in_tile, x_ref, g1_ref, w_ref, gqk_ref, cos_ref, sin_ref, *rest):
    o_ref, h_ref = rest[-8], rest[-1]
    k_rows, v_rows = rest[-7:-4], rest[-4:-1]
    j = pl.program_id(1)

    @pl.when(j == 0)
    def _():
        h_ref[...] = (_rms(x_ref[...]) * g1_ref[...]).astype(BF16)

    tm = x_ref.shape[0]
    rc = min(tm, ROW_CHUNK)

    def tile(rope, row_refs):
        for c in range(tm // rc):
            rs = slice(c * rc, (c + 1) * rc)
            for g in range(N_GROUPS):
                res = jnp.dot(h_ref[rs, :], w_ref[:, g * GROUP_WIDTH:(g + 1) * GROUP_WIDTH],
                              preferred_element_type=F32)
                first_kept = tm - keep_in_tile[g]
                lo = max(c * rc, first_kept)
                for h in range(ATTN_HEADS):
                    cs = slice(g * GROUP_WIDTH + h * HEAD_DIM, g * GROUP_WIDTH + (h + 1) * HEAD_DIM)
                    y = res[:, h * HEAD_DIM:(h + 1) * HEAD_DIM]
                    if rope:
                        y = _rms(y) * gqk_ref[:, cs]
                        y = y * cos_ref[rs, :] + pltpu.roll(y, HEAD_DIM // 2, 1) * sin_ref[rs, :]
                    o_ref[rs, cs] = y
                    if row_refs is not None and lo < (c + 1) * rc:
                        n = (c + 1) * rc - lo
                        dst = (lo - first_kept) * ATTN_HEADS + h
                        row_refs[g][pl.ds(dst, n, stride=ATTN_HEADS), :] = y[lo - c * rc:, :]

    pl.when(j == 0)(lambda: tile(True, None))
    pl.when(j == 1)(lambda: tile(True, k_rows))
    pl.when(j == 2)(lambda: tile(False, v_rows))


def _in_proj_qkv(x, g1, w, gqk, cos, sin, tm, layer, n_seq, keep, stacked):
    m = x.shape[0]
    n_pos_blocks = cos.shape[0] // tm
    tiles_per_seq = m // n_seq // tm
    keep_in_tile = tuple(min(k, tm) for k in keep)
    row_specs, row_shapes = [], []
    for k, kt in zip(keep, keep_in_tile):
        first_tile = tiles_per_seq - k // kt

        def index(i, j, first_tile=first_tile):
            return (layer, i // tiles_per_seq, jnp.maximum(i % tiles_per_seq - first_tile, 0), 0)

        row_specs.append(pl.BlockSpec((None, None, kt * ATTN_HEADS, HEAD_DIM), index))
        row_shapes.append(jax.ShapeDtypeStruct((DEPTH, n_seq, k * ATTN_HEADS, HEAD_DIM), F32))
    n_in = 6
    prev_specs = [] if stacked is None else [pl.BlockSpec(memory_space=pl.ANY)] * 6
    prev_args = [] if stacked is None else list(stacked)
    aliases = {} if stacked is None else {n_in + r: 1 + r for r in range(6)}
    outs = pl.pallas_call(
        functools.partial(_in_proj_qkv_kernel, keep_in_tile),
        grid=(m // tm, 3),
        in_specs=[
            pl.BlockSpec((tm, D_MODEL), lambda i, j: (i, 0)),
            pl.BlockSpec((1, D_MODEL), lambda i, j: (0, 0)),
            pl.BlockSpec((D_MODEL, QKV_WIDTH), lambda i, j: (0, j)),
            pl.BlockSpec((1, QKV_WIDTH), lambda i, j: (0, jnp.minimum(j, 1))),
            pl.BlockSpec((tm, HEAD_DIM), lambda i, j: (i % n_pos_blocks, 0)),
            pl.BlockSpec((tm, HEAD_DIM), lambda i, j: (i % n_pos_blocks, 0)),
        ] + prev_specs,
        out_specs=[pl.BlockSpec((tm, QKV_WIDTH), lambda i, j: (i, j))] + row_specs * 2,
        out_shape=[jax.ShapeDtypeStruct((m, 3 * QKV_WIDTH), F32)] + row_shapes * 2,
        input_output_aliases=aliases,
        scratch_shapes=[pltpu.VMEM((tm, D_MODEL), BF16)],
        compiler_params=_cparams(("arbitrary", "arbitrary")),
        name="in_proj_qkv",
    )(x, g1, w, gqk, cos, sin, *prev_args)
    return outs[0], outs[1:]


def _attn_prompt_kernel(q0, q1, q2, k0, k1, k2, v0, v1, v2, o_ref, m_sc, l_sc, a_sc):
    seq = o_ref.shape[0]
    nk = ATTN_NK
    row = lax.broadcasted_iota(jnp.int32, (nk, nk), 0)
    col = lax.broadcasted_iota(jnp.int32, (nk, nk), 1)
    mask_cur = col <= row
    mask_prev = col >= row

    def rows(start, stride):
        if stride == 1:
            return pl.ds(start, nk)
        return pl.ds(start, nk, stride=stride)

    def load(qkv, dil, starts):
        q_ref, k_ref, v_ref = qkv
        loaded = []
        for start, prev_start in starts:
            sl = rows(start, dil)
            item = [q_ref[sl, :], k_ref[sl, :], v_ref[sl, :]]
            if prev_start is not None:
                slp = rows(prev_start, dil)
                item += [k_ref[slp, :], v_ref[slp, :]]
            loaded.append(item)
        return loaded

    def attend(loaded):
        scored = []
        for item in loaded:
            q = item[0].astype(BF16)
            s = [jnp.where(mask_cur, _dot_nt(q, item[1].astype(BF16)) * ATTN_SCALE, NEG_BIG)]
            if len(item) > 3:
                s.append(jnp.where(mask_prev, _dot_nt(q, item[3].astype(BF16)) * ATTN_SCALE,
                                   NEG_BIG))
            scored.append(s)
        probs = []
        for s in scored:
            m = jnp.max(s[0] if len(s) == 1 else jnp.maximum(s[0], s[1]), axis=-1, keepdims=True)
            p = [jnp.exp(x - m) for x in s]
            l = jnp.sum(p[0] if len(s) == 1 else p[0] + p[1], axis=-1, keepdims=True)
            probs.append((m, l, [x.astype(BF16) for x in p]))
        results = []
        for item, (m, l, p) in zip(loaded, probs):
            acc = jnp.dot(p[0], item[2].astype(BF16), preferred_element_type=F32)
            if len(p) > 1:
                acc = acc + jnp.dot(p[1], item[4].astype(BF16), preferred_element_type=F32)
            results.append((m, l, acc))
        return results

    def keep(g, dil, starts, results):
        for (start, _), (m, l, acc) in zip(starts, results):
            sl = rows(start, dil)
            m_sc[g, sl, :] = jnp.broadcast_to(m, (nk, HEAD_DIM))
            l_sc[g, sl, :] = jnp.broadcast_to(l, (nk, HEAD_DIM))
            a_sc[g, sl, :] = acc

    def blocks(g, qkv, dil, starts):
        keep(g, dil, starts, attend(load(qkv, dil, starts)))

    refs = ((q0, k0, v0), (q1, k1, v1), (q2, k2, v2))
    unr = ATTN_UNROLL
    for g, (_, dil) in enumerate(ATTN_GROUPS):
        qkv = refs[g]
        nb = seq // dil // nk
        stride = nk * dil
        if nb == 1:
            def body(it, carry, g=g, qkv=qkv, dil=dil):
                blocks(g, qkv, dil, [(it * unr + u, None) for u in range(unr)])
                return carry
            lax.fori_loop(0, dil // unr, body, 0)
        elif dil == 1:
            blocks(g, qkv, dil, [(n * nk, None if n == 0 else (n - 1) * nk) for n in range(unr)])

            def body(it, carry, g=g, qkv=qkv, dil=dil):
                starts = [pl.multiple_of((it * unr + u) * nk, nk) for u in range(unr)]
                blocks(g, qkv, dil, [(s, s - nk) for s in starts])
                return carry
            lax.fori_loop(1, nb // unr, body, 0)
        else:
            blocks(g, qkv, dil, [(r, None) for r in range(dil)])

            def body(n, carry, g=g, qkv=qkv, dil=dil, stride=stride):
                blocks(g, qkv, dil, [(r + n * stride, r + (n - 1) * stride) for r in range(dil)])
                return carry
            lax.fori_loop(1, nb, body, 0)

    m_all = jnp.maximum(jnp.maximum(m_sc[0], m_sc[1]), m_sc[2])
    num = jnp.zeros((seq, HEAD_DIM), F32)
    den = jnp.zeros((seq, HEAD_DIM), F32)
    for g in range(N_GROUPS):
        w = jnp.exp(m_sc[g] - m_all)
        num = num + w * a_sc[g]
        den = den + w * l_sc[g]
    o_ref[...] = (num / den).astype(o_ref.dtype)


def _attn_prompt(qkv3):
    b, seq, _ = qkv3.shape

    def spec(col0, g):
        base = (col0 + g * GROUP_WIDTH) // HEAD_DIM
        return pl.BlockSpec((None, seq, HEAD_DIM), lambda bi, h, base=base: (bi, 0, base + h))

    in_specs = ([spec(COL_Q, g) for g in range(N_GROUPS)]
                + [spec(COL_K, g) for g in range(N_GROUPS)]
                + [spec(COL_V, g) for g in range(N_GROUPS)])
    return pl.pallas_call(
        _attn_prompt_kernel,
        grid=(b, ATTN_HEADS),
        in_specs=in_specs,
        out_specs=pl.BlockSpec((None, seq, HEAD_DIM), lambda bi, h: (bi, 0, h)),
        out_shape=jax.ShapeDtypeStruct((b, seq, GROUP_WIDTH), BF16),
        scratch_shapes=[pltpu.VMEM((N_GROUPS, seq, HEAD_DIM), F32)] * 3,
        compiler_params=_cparams(("parallel", "parallel")),
        name="attn_prompt",
    )(*([qkv3] * 9))


def _attn_sample_kernel(qkv_ref, kc0, vc0, kc1, vc1, kc2, vc2, o_ref):
    bb = o_ref.shape[0]
    caches = ((kc0, vc0), (kc1, vc1), (kc2, vc2))
    n_rows = ATTN_NK * ATTN_HEADS
    r8 = lax.broadcasted_iota(jnp.int32, (8, n_rows), 0)
    c8 = lax.broadcasted_iota(jnp.int32, (8, n_rows), 1)
    own = (r8 & (ATTN_HEADS - 1)) == (c8 & (ATTN_HEADS - 1))

    def window(cache_ref, b):
        return cache_ref[b].reshape(n_rows, HEAD_DIM).astype(BF16)

    def heads_on_rows(row, col0):
        hs = [row[:, col0 + h * HEAD_DIM:col0 + (h + 1) * HEAD_DIM] for h in range(ATTN_HEADS)]
        return jnp.concatenate(hs + hs, axis=0)

    def body(it, carry):
        items = []
        for u in range(SAMPLE_UNROLL):
            b = it * SAMPLE_UNROLL + u
            row = qkv_ref[b]
            for g in range(N_GROUPS):
                lo = g * GROUP_WIDTH
                q = heads_on_rows(row, COL_Q + lo)
                kn = heads_on_rows(row, COL_K + lo)
                vn = heads_on_rows(row, COL_V + lo)
                s = _dot_nt(q.astype(BF16), window(caches[g][0], b)) * ATTN_SCALE
                s = jnp.where(own, s, NEG_BIG)
                sn = jnp.sum(q * kn, axis=-1, keepdims=True) * ATTN_SCALE
                items.append((b, g, s, sn, vn))
        probs = []
        for b, g, s, sn, vn in items:
            m = jnp.maximum(jnp.max(s, axis=-1, keepdims=True), sn)
            p = jnp.exp(s - m)
            pn = jnp.exp(sn - m)
            probs.append((m, jnp.sum(p, axis=-1, keepdims=True) + pn, p.astype(BF16), pn))
        accs = [jnp.dot(p, window(caches[g][1], b), preferred_element_type=F32) + pn * vn
                for (b, g, _, _, vn), (_, _, p, pn) in zip(items, probs)]
        for u in range(SAMPLE_UNROLL):
            sl = slice(u * N_GROUPS, (u + 1) * N_GROUPS)
            ms = [pr[0] for pr in probs[sl]]
            ls = [pr[1] for pr in probs[sl]]
            m_all = jnp.maximum(jnp.maximum(ms[0], ms[1]), ms[2])
            num = jnp.zeros((8, HEAD_DIM), F32)
            den = jnp.zeros((8, 1), F32)
            for g in range(N_GROUPS):
                w = jnp.exp(ms[g] - m_all)
                num = num + w * accs[sl][g]
                den = den + w * ls[g]
            out = num / den
            o_ref[it * SAMPLE_UNROLL + u] = jnp.concatenate(
                [out[h:h + 1] for h in range(ATTN_HEADS)], axis=1)
        return carry

    lax.fori_loop(0, bb // SAMPLE_UNROLL, body, 0)


def _attn_sample(qkv3, caches, layer, bb):
    m = qkv3.shape[0]
    cache_specs = [pl.BlockSpec((None, bb, ATTN_NK, None, ATTN_HEADS, HEAD_DIM),
                                lambda i: (layer, i, 0, 0, 0, 0))
                   for _ in range(2 * N_GROUPS)]
    return pl.pallas_call(
        _attn_sample_kernel,
        grid=(m // bb,),
        in_specs=[pl.BlockSpec((bb, 1, 3 * QKV_WIDTH), lambda i: (i, 0, 0))] + cache_specs,
        out_specs=pl.BlockSpec((bb, 1, GROUP_WIDTH), lambda i: (i, 0, 0)),
        out_shape=jax.ShapeDtypeStruct((m, 1, GROUP_WIDTH), F32),
        compiler_params=_cparams(("parallel",)),
        name="attn_sample",
    )(qkv3, *caches)


def _cumsum8(y, rid):
    for s in (1, 2, 4):
        y = y + jnp.where(rid >= s, pltpu.roll(y, s, 0), 0.0)
    return y


def _hgrn_prompt_kernel(q_ref, fg_ref, v_ref, gate_ref, gn_ref, o_ref, sfin_ref, st_sc):
    t = pl.program_id(1)
    blk = HG_BLOCK
    c = HG_CHUNK
    nch = blk // c

    @pl.when(t == 0)
    def _():
        st_sc[...] = jnp.zeros_like(st_sc)

    rowi = lax.broadcasted_iota(jnp.int32, (blk, blk), 0)
    coli = lax.broadcasted_iota(jnp.int32, (blk, blk), 1)
    causal = coli <= rowi
    rid8 = lax.broadcasted_iota(jnp.int32, (8, HG_DK), 0)

    def head_scores(q, fg):
        q = q.astype(F32)
        lf = jnp.log(fg)
        kh = 1.0 - fg

        slabs = []
        for k in range(blk // 8):
            s8 = _cumsum8(lf[8 * k:8 * k + 8], rid8)
            if k % 2 == 1:
                s8 = s8 + slabs[k - 1][7:8, :]
            slabs.append(s8)
        g_pre = jnp.zeros((1, HG_DK), F32)
        g_starts, acum_slabs = [], []
        for i in range(nch):
            g_starts.append(g_pre)
            acum_slabs += [slabs[2 * i] + g_pre, slabs[2 * i + 1] + g_pre]
            g_pre = g_pre + slabs[2 * i + 1][7:8, :]
        g_end = g_pre
        a = jnp.concatenate(slabs, axis=0)
        acum = jnp.concatenate(acum_slabs, axis=0)

        qt = (q * jnp.exp(a)).astype(BF16)
        qhat = (q * jnp.exp(acum)).astype(BF16)
        khat = (kh * jnp.exp(g_end - acum)).astype(BF16)

        prow = []
        for i in range(nch):
            n = c * (i + 1)
            kt = (kh[:n] * jnp.exp(g_starts[i] - acum[:n])).astype(BF16)
            if n < blk:
                kt = jnp.concatenate([kt, jnp.zeros((blk - n, HG_DK), BF16)], axis=0)
            prow.append(_dot_nt(qt[c * i:c * (i + 1)], kt))
        return prow, qhat, khat, jnp.exp(g_end)

    def head_output(scores, v, gate, st0):
        prow, qhat, khat, dec = scores
        p = jnp.where(causal, jnp.concatenate(prow, axis=0), 0.0).astype(BF16)
        vt = v.astype(F32).T.astype(BF16)
        o = _dot_nt(jnp.concatenate([p, qhat], axis=1),
                    jnp.concatenate([vt, st0.astype(BF16)], axis=1))
        o = _rms(o) * gn_ref[...] * gate.astype(F32)
        return o, st0 * dec + jnp.dot(vt, khat, preferred_element_type=F32)

    def head_group(i, carry):
        heads = [i * HG_HEAD_UNROLL + u for u in range(HG_HEAD_UNROLL)]
        cols = [pl.ds(pl.multiple_of(h * HG_DK, HG_DK), HG_DK) for h in heads]
        loaded = [(q_ref[:, cs], fg_ref[:, cs], v_ref[:, cs], gate_ref[:, cs], st_sc[h])
                  for h, cs in zip(heads, cols)]
        scores = [head_scores(*args[:2]) for args in loaded]
        results = [head_output(sc, *args[2:]) for sc, args in zip(scores, loaded)]
        for h, cs, (o, st_new) in zip(heads, cols, results):
            o_ref[:, cs] = o.astype(o_ref.dtype)
            st_sc[h] = st_new
        return carry

    lax.fori_loop(0, HG_HEADS // HG_HEAD_UNROLL, head_group, 0)

    @pl.when(t == pl.num_programs(1) - 1)
    def _():
        for h in range(HG_HEADS):
            sfin_ref[h] = st_sc[h].T


def _hgrn_prompt(act3, fg3, gn):
    b, seq, _ = act3.shape

    def seg(blk):
        return pl.BlockSpec((None, HG_BLOCK, HG_WIDTH), lambda bi, t, blk=blk: (bi, t, blk))

    return pl.pallas_call(
        _hgrn_prompt_kernel,
        grid=(b, seq // HG_BLOCK),
        in_specs=[seg(ACT_Q), seg(0), seg(ACT_V), seg(ACT_GH),
                  pl.BlockSpec((1, HG_DV), lambda bi, t: (0, 0))],
        out_specs=[pl.BlockSpec((None, HG_BLOCK, HG_WIDTH), lambda bi, t: (bi, t, 0)),
                   pl.BlockSpec((None, HG_HEADS, HG_DK, HG_DV), lambda bi, t: (bi, 0, 0, 0))],
        out_shape=[jax.ShapeDtypeStruct((b, seq, HG_WIDTH), BF16),
                   jax.ShapeDtypeStruct((b, HG_HEADS, HG_DK, HG_DV), F32)],
        scratch_shapes=[pltpu.VMEM((HG_HEADS, HG_DV, HG_DK), F32)],
        compiler_params=_cparams(("parallel", "arbitrary")),
        name="hgrn_prompt",
    )(act3, fg3, act3, act3, gn)


def _hgrn_sample_kernel(q_ref, fg_ref, v_ref, gate_ref, gn_ref, s_ref, *rest):
    o_ref, sout_ref = rest[-2:]
    bb = q_ref.shape[0]
    ri = lax.broadcasted_iota(jnp.int32, (HG_DK, HG_DK), 0)
    ci = lax.broadcasted_iota(jnp.int32, (HG_DK, HG_DK), 1)
    diag = ri == ci

    def column(rowvec):
        full = jnp.broadcast_to(rowvec, (HG_DK, HG_DK))
        return jnp.sum(jnp.where(diag, full, 0.0), axis=1, keepdims=True)

    def body(b, carry):
        q = q_ref[b].astype(F32)
        fg = fg_ref[b]
        v = v_ref[b].astype(F32)
        gate = gate_ref[b].astype(F32)
        a = jnp.log(fg)
        kh = 1.0 - fg
        qt = q * jnp.exp(a)
        kt = kh * jnp.exp(-a)
        dec = jnp.exp(a)
        for h in range(HG_HEADS):
            hs = slice(h * HG_DK, (h + 1) * HG_DK)
            s0 = s_ref[b, h]
            att = jnp.sum(qt[:, hs] * kt[:, hs], axis=-1, keepdims=True)
            q8 = jnp.broadcast_to(qt[:, hs], (8, HG_DK)).astype(BF16)
            o = att * v[:, hs] + jnp.dot(q8, s0.astype(BF16), preferred_element_type=F32)[0:1]
            sout_ref[b, h] = column(dec[:, hs]) * s0 + column(kh[:, hs]) * v[:, hs]
            o_ref[b, :, hs] = _rms(o) * gn_ref[...] * gate[:, hs]
        return carry

    lax.fori_loop(0, bb, body, 0)


def _hgrn_sample(act3, fg3, gn, state, layer, bb, stacked):
    m = act3.shape[0]
    prev_specs = [] if stacked is None else [pl.BlockSpec(memory_space=pl.ANY)]
    prev_args = [] if stacked is None else [stacked]
    n_in = 6

    def seg(blk):
        return pl.BlockSpec((bb, 1, HG_WIDTH), lambda i, blk=blk: (i, 0, blk))

    return pl.pallas_call(
        _hgrn_sample_kernel,
        grid=(m // bb,),
        in_specs=[seg(ACT_Q), seg(0), seg(ACT_V), seg(ACT_GH),
                  pl.BlockSpec((1, HG_DV), lambda i: (0, 0)),
                  pl.BlockSpec((None, bb, HG_HEADS, HG_DK, HG_DV),
                               lambda i: (layer, i, 0, 0, 0))] + prev_specs,
        out_specs=[pl.BlockSpec((bb, 1, HG_WIDTH), lambda i: (i, 0, 0)),
                   pl.BlockSpec((None, bb, HG_HEADS, HG_DK, HG_DV),
                                lambda i: (layer, i, 0, 0, 0))],
        out_shape=[jax.ShapeDtypeStruct((m, 1, HG_WIDTH), F32),
                   jax.ShapeDtypeStruct((DEPTH, m, HG_HEADS, HG_DK, HG_DV), F32)],
        input_output_aliases={} if stacked is None else {n_in: 1},
        compiler_params=_cparams(("parallel",)),
        name="hgrn_sample",
    )(act3, fg3, act3, act3, gn, state, *prev_args)


def _post_kernel(x_ref, ya_ref, yb_ref, ga_ref, gb_ref, g2_ref, wpa_ref, wpb_ref, wo_ref,
                 wup_ref, wdn_ref, o_ref):
    tm = x_ref.shape[0]
    rc = min(tm, ROW_CHUNK)
    chunks = [slice(c * rc, (c + 1) * rc) for c in range(tm // rc)]
    dot = functools.partial(jnp.dot, preferred_element_type=F32)

    branches = [(dot(ya_ref[rs, :].astype(BF16), wpa_ref[...]),
                 dot(yb_ref[rs, :].astype(BF16), wpb_ref[...])) for rs in chunks]
    x1 = []
    for rs, (pa, pb) in zip(chunks, branches):
        merged = ga_ref[rs, :].astype(F32) * pa + gb_ref[rs, :].astype(F32) * pb
        x1.append(x_ref[rs, :] + dot(merged.astype(BF16), wo_ref[...]))
    up = [dot((_rms(x) * g2_ref[...]).astype(BF16), wup_ref[...]) for x in x1]
    for rs, x, u in zip(chunks, x1, up):
        act = jnp.square(jnp.maximum(u, 0.0)).astype(BF16)
        o_ref[rs, :] = x + dot(act, wdn_ref[...])


def _post(x, ya, yb, act, g2, wpa, wpb, wo, wup, wdn, tm):
    m = x.shape[0]
    row = lambda w: pl.BlockSpec((tm, w), lambda i: (i, 0))
    resident = lambda a: pl.BlockSpec(a.shape, lambda i: (0, 0), pipeline_mode=pl.Buffered(1))
    return pl.pallas_call(
        _post_kernel,
        grid=(m // tm,),
        in_specs=[row(D_MODEL), row(GROUP_WIDTH), row(HG_WIDTH),
                  pl.BlockSpec((tm, HG_WIDTH), lambda i: (i, ACT_GA)),
                  pl.BlockSpec((tm, HG_WIDTH), lambda i: (i, ACT_GB)),
                  pl.BlockSpec((1, D_MODEL), lambda i: (0, 0)),
                  resident(wpa), resident(wpb), resident(wo), resident(wup), resident(wdn)],
        out_specs=row(D_MODEL),
        out_shape=jax.ShapeDtypeStruct((m, D_MODEL), F32),
        compiler_params=pltpu.CompilerParams(dimension_semantics=("parallel",),
                                             vmem_limit_bytes=POST_VMEM_LIMIT),
        name="post",
    )(x, ya, yb, act, act, g2, wpa, wpb, wo, wup, wdn)


def _rope_tables(pos):
    half = HEAD_DIM // 2
    inv = ROPE_THETA ** (-(jnp.arange(half, dtype=F32) * 2.0 / HEAD_DIM))
    ang = pos.astype(F32)[:, None] * inv[None, :]
    cos, sin = jnp.cos(ang), jnp.sin(ang)
    return jnp.concatenate([cos, cos], axis=-1), jnp.concatenate([-sin, sin], axis=-1)


def kernel(x_prompt, x_sample, cache_k_w128, cache_v_w128, cache_k_w512, cache_v_w512,
           cache_k_w2048, cache_v_w2048, state_hgrn, w_in, norm1, q_norm, k_norm, hg_out_norm,
           hg_lower_bounds, w_pa, w_pb, w_o, norm2, w_up, w_down):
    bsz, seq, _ = x_prompt.shape
    dec_b, dec_t, _ = x_sample.shape
    mp = bsz * seq
    ms = dec_b * dec_t

    lb_all = jnp.cumsum(jax.nn.softmax(hg_lower_bounds.astype(F32), axis=0), axis=0)
    lb_all = lb_all - lb_all[0:1]

    cos_p, sin_p = _rope_tables(jnp.arange(seq))
    cos_s, sin_s = _rope_tables(jnp.broadcast_to(PAST_LEN + jnp.arange(dec_t), (ms,)))

    caches = []
    for (win, dil), (ck, cv) in zip(ATTN_GROUPS, ((cache_k_w128, cache_v_w128),
                                                   (cache_k_w512, cache_v_w512),
                                                   (cache_k_w2048, cache_v_w2048))):
        wc = ck.shape[2]
        shape = (DEPTH, dec_b, wc // dil, dil, ATTN_HEADS, HEAD_DIM)
        caches += [ck.reshape(shape), cv.reshape(shape)]

    w_in_b, w_pa_b, w_pb_b, w_o_b, w_up_b, w_dn_b = (
        w.astype(BF16) for w in (w_in, w_pa, w_pb, w_o, w_up, w_down))

    yp = x_prompt.reshape(mp, D_MODEL)
    ys = x_sample.reshape(ms, D_MODEL)
    keep_p = tuple(min(win, seq) for win, _ in ATTN_GROUPS)
    p_rows, s_rows = None, None
    p_states, s_states = [], None

    for layer in range(DEPTH):
        w_in_l = w_in_b[layer]
        wpa, wpb, wo = w_pa_b[layer], w_pb_b[layer], w_o_b[layer]
        wup, wdn = w_up_b[layer], w_dn_b[layer]
        g1 = norm1[layer].reshape(1, D_MODEL)
        g2 = norm2[layer].reshape(1, D_MODEL)
        gqk = jnp.concatenate([jnp.tile(q_norm[layer], (1, ATTN_HEADS)).reshape(1, QKV_WIDTH),
                               jnp.tile(k_norm[layer], (1, ATTN_HEADS)).reshape(1, QKV_WIDTH)],
                              axis=-1)
        gn = hg_out_norm[layer].reshape(1, HG_DV)
        lb = lb_all[layer].reshape(1, HG_WIDTH)

        act, fg = _in_proj_gate(yp, g1, w_in_l, lb, tm=1024)
        qkv, p_rows = _in_proj_qkv(yp, g1, w_in_l, gqk, cos_p, sin_p, tm=1024, layer=layer,
                                   n_seq=bsz, keep=keep_p, stacked=p_rows)
        ya = _attn_prompt(qkv.reshape(bsz, seq, 3 * QKV_WIDTH))
        yb, st = _hgrn_prompt(act.reshape(bsz, seq, ACT_WIDTH), fg.reshape(bsz, seq, HG_WIDTH), gn)
        yp = _post(yp, ya.reshape(mp, GROUP_WIDTH), yb.reshape(mp, HG_WIDTH), act, g2,
                   wpa, wpb, wo, wup, wdn, tm=512)
        p_states.append(st)

        act_s, fg_s = _in_proj_gate(ys, g1, w_in_l, lb, tm=ms)
        qkv_s, s_rows = _in_proj_qkv(ys, g1, w_in_l, gqk, cos_s, sin_s, tm=ms, layer=layer,
                                     n_seq=1, keep=(ms,) * N_GROUPS, stacked=s_rows)
        ya_s = _attn_sample(qkv_s.reshape(ms, 1, 3 * QKV_WIDTH), caches, layer, bb=8)
        yb_s, s_states = _hgrn_sample(act_s.reshape(ms, 1, ACT_WIDTH), fg_s.reshape(ms, 1, HG_WIDTH),
                                      gn, state_hgrn, layer, bb=8, stacked=s_states)
        ys = _post(ys, ya_s.reshape(ms, GROUP_WIDTH), yb_s.reshape(ms, HG_WIDTH), act_s, g2,
                   wpa, wpb, wo, wup, wdn, tm=ms)

    def interleave_kv(rows, n, keep):
        return [rows[c * N_GROUPS + g].reshape(DEPTH, n, keep[g], ATTN_HEADS, HEAD_DIM)
                for g in range(N_GROUPS) for c in range(2)]

    outs = [yp.reshape(bsz, seq, D_MODEL), ys.reshape(dec_b, dec_t, D_MODEL)]
    outs += interleave_kv(p_rows, bsz, keep_p)
    outs.append(jnp.stack(p_states, axis=0))
    outs += interleave_kv(s_rows, dec_b, (dec_t,) * N_GROUPS)
    outs.append(s_states)
    return tuple(outs)
```

```python
import functools

import jax
import jax.numpy as jnp
from jax import lax
from jax.experimental import pallas as pl
from jax.experimental.pallas import tpu as pltpu

F32 = jnp.float32
BF16 = jnp.bfloat16

D_MODEL = 1024
DEPTH = 4
PAST_LEN = 2048
ATTN_GROUPS = ((128, 1), (512, 4), (2048, 16))
N_GROUPS = 3
ATTN_HEADS = 4
HEAD_DIM = 128
GROUP_WIDTH = ATTN_HEADS * HEAD_DIM
QKV_WIDTH = N_GROUPS * GROUP_WIDTH
ATTN_NK = 128
ATTN_SCALE = HEAD_DIM ** -0.5
HG_HEADS = 8
HG_DK = 128
HG_DV = 128
HG_WIDTH = 1024
HG_CHUNK = 16
HG_BLOCK = 128
HG_HEAD_UNROLL = 4
D_FF = 4 * D_MODEL
ROPE_THETA = 10000.0
NORM_EPS = 1e-6
IN_WIDTH = 3 * QKV_WIDTH + 6 * HG_WIDTH
GATE_WIDTH = 6 * HG_WIDTH
NEG_BIG = -1e30

SEG_QH, SEG_FH, SEG_IH, SEG_GH, SEG_GA, SEG_GB = range(6)
ACT_Q, ACT_V, ACT_GH, ACT_GA, ACT_GB = range(5)
ACT_WIDTH = 5 * HG_WIDTH
COL_Q, COL_K, COL_V = 0, QKV_WIDTH, 2 * QKV_WIDTH

GATE_TILE = 1536
ROW_CHUNK = 256
ATTN_UNROLL = 8
SAMPLE_UNROLL = 2

VMEM_LIMIT = 48 * 1024 * 1024
POST_VMEM_LIMIT = 56 * 1024 * 1024


def _cparams(sem):
    return pltpu.CompilerParams(dimension_semantics=sem, vmem_limit_bytes=VMEM_LIMIT)


def _rms(x):
    return x * lax.rsqrt(jnp.mean(x * x, axis=-1, keepdims=True) + NORM_EPS)


def _dot_nt(a, b):
    return lax.dot_general(a, b, (((1,), (1,)), ((), ())), preferred_element_type=F32)


def _in_proj_gate_kernel(x_ref, g1_ref, w_ref, lb_ref, act_ref, fg_ref, h_ref):
    j = pl.program_id(1)

    @pl.when(j == 0)
    def _():
        h_ref[...] = (_rms(x_ref[...]) * g1_ref[...]).astype(BF16)

    tm = x_ref.shape[0]
    rc = min(tm, ROW_CHUNK)
    piece = GROUP_WIDTH
    act_slot = {SEG_QH: ACT_Q, SEG_IH: ACT_V, SEG_GH: ACT_GH, SEG_GA: ACT_GA, SEG_GB: ACT_GB}

    def tile(jj):
        for c in range(tm // rc):
            rs = slice(c * rc, (c + 1) * rc)
            res = jnp.dot(h_ref[rs, :], w_ref[...], preferred_element_type=F32)
            for p in range(GATE_TILE // piece):
                col = jj * GATE_TILE + p * piece
                seg, off = col // HG_WIDTH, col % HG_WIDTH
                y = res[:, p * piece:(p + 1) * piece]
                if seg == SEG_FH:
                    lb = lb_ref[:, off:off + piece]
                    fg_ref[rs, off:off + piece] = lb + (1.0 - lb) * jax.nn.sigmoid(y)
                    continue
                if seg == SEG_QH:
                    y = jax.nn.silu(y)
                elif seg != SEG_IH:
                    y = jax.nn.sigmoid(y)
                dst = act_slot[seg] * HG_WIDTH + off
                act_ref[rs, dst:dst + piece] = y.astype(BF16)

    for jj in range(GATE_WIDTH // GATE_TILE):
        pl.when(j == jj)(functools.partial(tile, jj))


def _in_proj_gate(x, g1, w, lb, tm, layer):
    m = x.shape[0]
    first = 3 * QKV_WIDTH // GATE_TILE
    return pl.pallas_call(
        _in_proj_gate_kernel,
        grid=(m // tm, GATE_WIDTH // GATE_TILE),
        in_specs=[
            pl.BlockSpec((tm, D_MODEL), lambda i, j: (i, 0)),
            pl.BlockSpec((1, D_MODEL), lambda i, j: (0, 0)),
            pl.BlockSpec((None, D_MODEL, GATE_TILE), lambda i, j: (layer, 0, first + j)),
            pl.BlockSpec((1, HG_WIDTH), lambda i, j: (0, 0)),
        ],
        out_specs=[pl.BlockSpec((tm, ACT_WIDTH), lambda i, j: (i, 0)),
                   pl.BlockSpec((tm, HG_WIDTH), lambda i, j: (i, 0))],
        out_shape=[jax.ShapeDtypeStruct((m, ACT_WIDTH), BF16),
                   jax.ShapeDtypeStruct((m, HG_WIDTH), F32)],
        scratch_shapes=[pltpu.VMEM((tm, D_MODEL), BF16)],
        compiler_params=_cparams(("parallel", "arbitrary")),
        name="in_proj_gate",
    )(x, g1, w, lb)


def _in_proj_qkv_kernel(keep_in_tile, x_ref, g1_ref, w_ref, gqk_ref, cos_ref, sin_ref, *rest):
    o_ref, h_ref = rest[-8], rest[-1]
    k_rows, v_rows = rest[-7:-4], rest[-4:-1]
    j = pl.program_id(1)

    @pl.when(j == 0)
    def _():
        h_ref[...] = (_rms(x_ref[...]) * g1_ref[...]).astype(BF16)

    tm = x_ref.shape[0]
    rc = min(tm, ROW_CHUNK)

    def tile(rope, row_refs):
        for c in range(tm // rc):
            rs = slice(c * rc, (c + 1) * rc)
            for g in range(N_GROUPS):
                res = jnp.dot(h_ref[rs, :], w_ref[:, g * GROUP_WIDTH:(g + 1) * GROUP_WIDTH],
                              preferred_element_type=F32)
                first_kept = tm - keep_in_tile[g]
                lo = max(c * rc, first_kept)
                for h in range(ATTN_HEADS):
                    cs = slice(g * GROUP_WIDTH + h * HEAD_DIM, g * GROUP_WIDTH + (h + 1) * HEAD_DIM)
                    y = res[:, h * HEAD_DIM:(h + 1) * HEAD_DIM]
                    if rope:
                        y = _rms(y) * gqk_ref[:, cs]
                        y = y * cos_ref[rs, :] + pltpu.roll(y, HEAD_DIM // 2, 1) * sin_ref[rs, :]
                    o_ref[rs, cs] = y
                    if row_refs is not None and lo < (c + 1) * rc:
                        n = (c + 1) * rc - lo
                        dst = (lo - first_kept) * ATTN_HEADS + h
                        row_refs[g][pl.ds(dst, n, stride=ATTN_HEADS), :] = y[lo - c * rc:, :]

    pl.when(j == 0)(lambda: tile(True, None))
    pl.when(j == 1)(lambda: tile(True, k_rows))
    pl.when(j == 2)(lambda: tile(False, v_rows))


def _in_proj_qkv(x, g1, w, gqk, cos, sin, tm, layer, n_seq, keep, stacked):
    m = x.shape[0]
    n_pos_blocks = cos.shape[0] // tm
    tiles_per_seq = m // n_seq // tm
    keep_in_tile = tuple(min(k, tm) for k in keep)
    row_specs, row_shapes = [], []
    for k, kt in zip(keep, keep_in_tile):
        first_tile = tiles_per_seq - k // kt

        def index(i, j, first_tile=first_tile):
            return (layer, i // tiles_per_seq, jnp.maximum(i % tiles_per_seq - first_tile, 0), 0)

        row_specs.append(pl.BlockSpec((None, None, kt * ATTN_HEADS, HEAD_DIM), index))
        row_shapes.append(jax.ShapeDtypeStruct((DEPTH, n_seq, k * ATTN_HEADS, HEAD_DIM), F32))
    n_in = 6
    prev_specs = [] if stacked is None else [pl.BlockSpec(memory_space=pl.ANY)] * 6
    prev_args = [] if stacked is None else list(stacked)
    aliases = {} if stacked is None else {n_in + r: 1 + r for r in range(6)}
    outs = pl.pallas_call(
        functools.partial(_in_proj_qkv_kernel, keep_in_tile),
        grid=(m // tm, 3),
        in_specs=[
            pl.BlockSpec((tm, D_MODEL), lambda i, j: (i, 0)),
            pl.BlockSpec((1, D_MODEL), lambda i, j: (0, 0)),
            pl.BlockSpec((None, D_MODEL, QKV_WIDTH), lambda i, j: (layer, 0, j)),
            pl.BlockSpec((1, QKV_WIDTH), lambda i, j: (0, jnp.minimum(j, 1))),
            pl.BlockSpec((tm, HEAD_DIM), lambda i, j: (i % n_pos_blocks, 0)),
            pl.BlockSpec((tm, HEAD_DIM), lambda i, j: (i % n_pos_blocks, 0)),
        ] + prev_specs,
        out_specs=[pl.BlockSpec((tm, QKV_WIDTH), lambda i, j: (i, j))] + row_specs * 2,
        out_shape=[jax.ShapeDtypeStruct((m, 3 * QKV_WIDTH), F32)] + row_shapes * 2,
        input_output_aliases=aliases,
        scratch_shapes=[pltpu.VMEM((tm, D_MODEL), BF16)],
        compiler_params=_cparams(("arbitrary", "arbitrary")),
        name="in_proj_qkv",
    )(x, g1, w, gqk, cos, sin, *prev_args)
    return outs[0], outs[1:]


def _attn_prompt_kernel(q0, q1, q2, k0, k1, k2, v0, v1, v2, o_ref, m_sc, l_sc, a_sc):
    seq = o_ref.shape[0]
    nk = ATTN_NK
    row = lax.broadcasted_iota(jnp.int32, (nk, nk), 0)
    col = lax.broadcasted_iota(jnp.int32, (nk, nk), 1)
    mask_cur = col <= row
    mask_prev = col >= row

    def rows(start, stride):
        if stride == 1:
            return pl.ds(start, nk)
        return pl.ds(start, nk, stride=stride)

    def load(qkv, dil, starts):
        q_ref, k_ref, v_ref = qkv
        loaded = []
        for start, prev_start in starts:
            sl = rows(start, dil)
            item = [q_ref[sl, :], k_ref[sl, :], v_ref[sl, :]]
            if prev_start is not None:
                slp = rows(prev_start, dil)
                item += [k_ref[slp, :], v_ref[slp, :]]
            loaded.append(item)
        return loaded

    def attend(loaded):
        scored = []
        for item in loaded:
            q = item[0].astype(BF16)
            s = [jnp.where(mask_cur, _dot_nt(q, item[1].astype(BF16)) * ATTN_SCALE, NEG_BIG)]
            if len(item) > 3:
                s.append(jnp.where(mask_prev, _dot_nt(q, item[3].astype(BF16)) * ATTN_SCALE,
                                   NEG_BIG))
            scored.append(s)
        probs = []
        for s in scored:
            m = jnp.max(s[0] if len(s) == 1 else jnp.maximum(s[0], s[1]), axis=-1, keepdims=True)
            p = [jnp.exp(x - m) for x in s]
            l = jnp.sum(p[0] if len(s) == 1 else p[0] + p[1], axis=-1, keepdims=True)
            probs.append((m, l, [x.astype(BF16) for x in p]))
        results = []
        for item, (m, l, p) in zip(loaded, probs):
            acc = jnp.dot(p[0], item[2].astype(BF16), preferred_element_type=F32)
            if len(p) > 1:
                acc = acc + jnp.dot(p[1], item[4].astype(BF16), preferred_element_type=F32)
            results.append((m, l, acc))
        return results

    def keep(g, dil, starts, results):
        for (start, _), (m, l, acc) in zip(starts, results):
            sl = rows(start, dil)
            m_sc[g, sl, :] = jnp.broadcast_to(m, (nk, HEAD_DIM))
            l_sc[g, sl, :] = jnp.broadcast_to(l, (nk, HEAD_DIM))
            a_sc[g, sl, :] = acc

    def blocks(g, qkv, dil, starts):
        keep(g, dil, starts, attend(load(qkv, dil, starts)))

    refs = ((q0, k0, v0), (q1, k1, v1), (q2, k2, v2))
    unr = ATTN_UNROLL
    for g, (_, dil) in enumerate(ATTN_GROUPS):
        qkv = refs[g]
        nb = seq // dil // nk
        stride = nk * dil
        if nb == 1:
            def body(it, carry, g=g, qkv=qkv, dil=dil):
                blocks(g, qkv, dil, [(it * unr + u, None) for u in range(unr)])
                return carry
            lax.fori_loop(0, dil // unr, body, 0)
        elif dil == 1:
            blocks(g, qkv, dil, [(n * nk, None if n == 0 else (n - 1) * nk) for n in range(unr)])

            def body(it, carry, g=g, qkv=qkv, dil=dil):
                starts = [pl.multiple_of((it * unr + u) * nk, nk) for u in range(unr)]
                blocks(g, qkv, dil, [(s, s - nk) for s in starts])
                return carry
            lax.fori_loop(1, nb // unr, body, 0)
        else:
            blocks(g, qkv, dil, [(r, None) for r in range(dil)])

            def body(n, carry, g=g, qkv=qkv, dil=dil, stride=stride):
                blocks(g, qkv, dil, [(r + n * stride, r + (n - 1) * stride) for r in range(dil)])
                return carry
            lax.fori_loop(1, nb, body, 0)

    m_all = jnp.maximum(jnp.maximum(m_sc[0], m_sc[1]), m_sc[2])
    num = jnp.zeros((seq, HEAD_DIM), F32)
    den = jnp.zeros((seq, HEAD_DIM), F32)
    for g in range(N_GROUPS):
        w = jnp.exp(m_sc[g] - m_all)
        num = num + w * a_sc[g]
        den = den + w * l_sc[g]
    o_ref[...] = (num / den).astype(o_ref.dtype)


def _attn_prompt(qkv3):
    b, seq, _ = qkv3.shape

    def spec(col0, g):
        base = (col0 + g * GROUP_WIDTH) // HEAD_DIM
        return pl.BlockSpec((None, seq, HEAD_DIM), lambda bi, h, base=base: (bi, 0, base + h))

    in_specs = ([spec(COL_Q, g) for g in range(N_GROUPS)]
                + [spec(COL_K, g) for g in range(N_GROUPS)]
                + [spec(COL_V, g) for g in range(N_GROUPS)])
    return pl.pallas_call(
        _attn_prompt_kernel,
        grid=(b, ATTN_HEADS),
        in_specs=in_specs,
        out_specs=pl.BlockSpec((None, seq, HEAD_DIM), lambda bi, h: (bi, 0, h)),
        out_shape=jax.ShapeDtypeStruct((b, seq, GROUP_WIDTH), BF16),
        scratch_shapes=[pltpu.VMEM((N_GROUPS, seq, HEAD_DIM), F32)] * 3,
        compiler_params=_cparams(("parallel", "parallel")),
        name="attn_prompt",
    )(*([qkv3] * 9))


def _attn_sample_kernel(qkv_ref, kc0, vc0, kc1, vc1, kc2, vc2, o_ref):
    bb = o_ref.shape[0]
    caches = ((kc0, vc0), (kc1, vc1), (kc2, vc2))
    n_rows = ATTN_NK * ATTN_HEADS
    r8 = lax.broadcasted_iota(jnp.int32, (8, n_rows), 0)
    c8 = lax.broadcasted_iota(jnp.int32, (8, n_rows), 1)
    own = (r8 & (ATTN_HEADS - 1)) == (c8 & (ATTN_HEADS - 1))

    def window(cache_ref, b):
        return cache_ref[b].reshape(n_rows, HEAD_DIM).astype(BF16)

    def heads_on_rows(row, col0):
        hs = [row[:, col0 + h * HEAD_DIM:col0 + (h + 1) * HEAD_DIM] for h in range(ATTN_HEADS)]
        return jnp.concatenate(hs + hs, axis=0)

    def body(it, carry):
        items = []
        for u in range(SAMPLE_UNROLL):
            b = it * SAMPLE_UNROLL + u
            row = qkv_ref[b]
            for g in range(N_GROUPS):
                lo = g * GROUP_WIDTH
                q = heads_on_rows(row, COL_Q + lo)
                kn = heads_on_rows(row, COL_K + lo)
                vn = heads_on_rows(row, COL_V + lo)
                s = _dot_nt(q.astype(BF16), window(caches[g][0], b)) * ATTN_SCALE
                s = jnp.where(own, s, NEG_BIG)
                sn = jnp.sum(q * kn, axis=-1, keepdims=True) * ATTN_SCALE
                items.append((b, g, s, sn, vn))
        probs = []
        for b, g, s, sn, vn in items:
            m = jnp.maximum(jnp.max(s, axis=-1, keepdims=True), sn)
            p = jnp.exp(s - m)
            pn = jnp.exp(sn - m)
            probs.append((m, jnp.sum(p, axis=-1, keepdims=True) + pn, p.astype(BF16), pn))
        accs = [jnp.dot(p, window(caches[g][1], b), preferred_element_type=F32) + pn * vn
                for (b, g, _, _, vn), (_, _, p, pn) in zip(items, probs)]
        for u in range(SAMPLE_UNROLL):
            sl = slice(u * N_GROUPS, (u + 1) * N_GROUPS)
            ms = [pr[0] for pr in probs[sl]]
            ls = [pr[1] for pr in probs[sl]]
            m_all = jnp.maximum(jnp.maximum(ms[0], ms[1]), ms[2])
            num = jnp.zeros((8, HEAD_DIM), F32)
            den = jnp.zeros((8, 1), F32)
            for g in range(N_GROUPS):
                w = jnp.exp(ms[g] - m_all)
                num = num + w * accs[sl][g]
                den = den + w * ls[g]
            out = num / den
            o_ref[it * SAMPLE_UNROLL + u] = jnp.concatenate(
                [out[h:h + 1] for h in range(ATTN_HEADS)], axis=1)
        return carry

    lax.fori_loop(0, bb // SAMPLE_UNROLL, body, 0)


def _attn_sample(qkv3, caches, layer, bb):
    m = qkv3.shape[0]
    cache_specs = [pl.BlockSpec((None, bb, ATTN_NK, None, ATTN_HEADS, HEAD_DIM),
                                lambda i: (layer, i, 0, 0, 0, 0))
                   for _ in range(2 * N_GROUPS)]
    return pl.pallas_call(
        _attn_sample_kernel,
        grid=(m // bb,),
        in_specs=[pl.BlockSpec((bb, 1, 3 * QKV_WIDTH), lambda i: (i, 0, 0))] + cache_specs,
        out_specs=pl.BlockSpec((bb, 1, GROUP_WIDTH), lambda i: (i, 0, 0)),
        out_shape=jax.ShapeDtypeStruct((m, 1, GROUP_WIDTH), F32),
        compiler_params=_cparams(("parallel",)),
        name="attn_sample",
    )(qkv3, *caches)


def _cumsum8(y, rid):
    for s in (1, 2, 4):
        y = y + jnp.where(rid >= s, pltpu.roll(y, s, 0), 0.0)
    return y


def _hgrn_prompt_kernel(q_ref, fg_ref, v_ref, gate_ref, gn_ref, o_ref, sfin_ref, st_sc):
    t = pl.program_id(1)
    blk = HG_BLOCK
    c = HG_CHUNK
    nch = blk // c

    @pl.when(t == 0)
    def _():
        st_sc[...] = jnp.zeros_like(st_sc)

    rowi = lax.broadcasted_iota(jnp.int32, (blk, blk), 0)
    coli = lax.broadcasted_iota(jnp.int32, (blk, blk), 1)
    causal = coli <= rowi
    rid8 = lax.broadcasted_iota(jnp.int32, (8, HG_DK), 0)

    def head_scores(q, fg):
        q = q.astype(F32)
        lf = jnp.log(fg)
        kh = 1.0 - fg

        slabs = []
        for k in range(blk // 8):
            s8 = _cumsum8(lf[8 * k:8 * k + 8], rid8)
            if k % 2 == 1:
                s8 = s8 + slabs[k - 1][7:8, :]
            slabs.append(s8)
        g_pre = jnp.zeros((1, HG_DK), F32)
        g_starts, acum_slabs = [], []
        for i in range(nch):
            g_starts.append(g_pre)
            acum_slabs += [slabs[2 * i] + g_pre, slabs[2 * i + 1] + g_pre]
            g_pre = g_pre + slabs[2 * i + 1][7:8, :]
        g_end = g_pre
        a = jnp.concatenate(slabs, axis=0)
        acum = jnp.concatenate(acum_slabs, axis=0)

        qt = (q * jnp.exp(a)).astype(BF16)
        qhat = (q * jnp.exp(acum)).astype(BF16)
        khat = (kh * jnp.exp(g_end - acum)).astype(BF16)

        prow = []
        for i in range(nch):
            n = c * (i + 1)
            kt = (kh[:n] * jnp.exp(g_starts[i] - acum[:n])).astype(BF16)
            if n < blk:
                kt = jnp.concatenate([kt, jnp.zeros((blk - n, HG_DK), BF16)], axis=0)
            prow.append(_dot_nt(qt[c * i:c * (i + 1)], kt))
        return prow, qhat, khat, jnp.exp(g_end)

    def head_output(scores, v, gate, st0):
        prow, qhat, khat, dec = scores
        p = jnp.where(causal, jnp.concatenate(prow, axis=0), 0.0).astype(BF16)
        vt = v.astype(F32).T.astype(BF16)
        o = _dot_nt(jnp.concatenate([p, qhat], axis=1),
                    jnp.concatenate([vt, st0.astype(BF16)], axis=1))
        o = _rms(o) * gn_ref[...] * gate.astype(F32)
        return o, st0 * dec + jnp.dot(vt, khat, preferred_element_type=F32)

    def head_group(i, carry):
        heads = [i * HG_HEAD_UNROLL + u for u in range(HG_HEAD_UNROLL)]
        cols = [pl.ds(pl.multiple_of(h * HG_DK, HG_DK), HG_DK) for h in heads]
        loaded = [(q_ref[:, cs], fg_ref[:, cs], v_ref[:, cs], gate_ref[:, cs], st_sc[h])
                  for h, cs in zip(heads, cols)]
        scores = [head_scores(*args[:2]) for args in loaded]
        results = [head_output(sc, *args[2:]) for sc, args in zip(scores, loaded)]
        for h, cs, (o, st_new) in zip(heads, cols, results):
            o_ref[:, cs] = o.astype(o_ref.dtype)
            st_sc[h] = st_new
        return carry

    lax.fori_loop(0, HG_HEADS // HG_HEAD_UNROLL, head_group, 0)

    @pl.when(t == pl.num_programs(1) - 1)
    def _():
        for h in range(HG_HEADS):
            sfin_ref[h] = st_sc[h].T


def _hgrn_prompt(act3, fg3, gn):
    b, seq, _ = act3.shape

    def seg(blk):
        return pl.BlockSpec((None, HG_BLOCK, HG_WIDTH), lambda bi, t, blk=blk: (bi, t, blk))

    return pl.pallas_call(
        _hgrn_prompt_kernel,
        grid=(b, seq // HG_BLOCK),
        in_specs=[seg(ACT_Q), seg(0), seg(ACT_V), seg(ACT_GH),
                  pl.BlockSpec((1, HG_DV), lambda bi, t: (0, 0))],
        out_specs=[pl.BlockSpec((None, HG_BLOCK, HG_WIDTH), lambda bi, t: (bi, t, 0)),
                   pl.BlockSpec((None, HG_HEADS, HG_DK, HG_DV), lambda bi, t: (bi, 0, 0, 0))],
        out_shape=[jax.ShapeDtypeStruct((b, seq, HG_WIDTH), BF16),
                   jax.ShapeDtypeStruct((b, HG_HEADS, HG_DK, HG_DV), F32)],
        scratch_shapes=[pltpu.VMEM((HG_HEADS, HG_DV, HG_DK), F32)],
        compiler_params=_cparams(("parallel", "arbitrary")),
        name="hgrn_prompt",
    )(act3, fg3, act3, act3, gn)


def _hgrn_sample_kernel(q_ref, fg_ref, v_ref, gate_ref, gn_ref, s_ref, *rest):
    o_ref, sout_ref = rest[-2:]
    bb = q_ref.shape[0]
    ri = lax.broadcasted_iota(jnp.int32, (HG_DK, HG_DK), 0)
    ci = lax.broadcasted_iota(jnp.int32, (HG_DK, HG_DK), 1)
    diag = ri == ci

    def column(rowvec):
        full = jnp.broadcast_to(rowvec, (HG_DK, HG_DK))
        return jnp.sum(jnp.where(diag, full, 0.0), axis=1, keepdims=True)

    def body(b, carry):
        q = q_ref[b].astype(F32)
        fg = fg_ref[b]
        v = v_ref[b].astype(F32)
        gate = gate_ref[b].astype(F32)
        a = jnp.log(fg)
        kh = 1.0 - fg
        qt = q * jnp.exp(a)
        kt = kh * jnp.exp(-a)
        dec = jnp.exp(a)
        for h in range(HG_HEADS):
            hs = slice(h * HG_DK, (h + 1) * HG_DK)
            s0 = s_ref[b, h]
            att = jnp.sum(qt[:, hs] * kt[:, hs], axis=-1, keepdims=True)
            q8 = jnp.broadcast_to(qt[:, hs], (8, HG_DK)).astype(BF16)
            o = att * v[:, hs] + jnp.dot(q8, s0.astype(BF16), preferred_element_type=F32)[0:1]
            sout_ref[b, h] = column(dec[:, hs]) * s0 + column(kh[:, hs]) * v[:, hs]
            o_ref[b, :, hs] = _rms(o) * gn_ref[...] * gate[:, hs]
        return carry

    lax.fori_loop(0, bb, body, 0)


def _hgrn_sample(act3, fg3, gn, state, layer, bb, stacked):
    m = act3.shape[0]
    prev_specs = [] if stacked is None else [pl.BlockSpec(memory_space=pl.ANY)]
    prev_args = [] if stacked is None else [stacked]
    n_in = 6

    def seg(blk):
        return pl.BlockSpec((bb, 1, HG_WIDTH), lambda i, blk=blk: (i, 0, blk))

    return pl.pallas_call(
        _hgrn_sample_kernel,
        grid=(m // bb,),
        in_specs=[seg(ACT_Q), seg(0), seg(ACT_V), seg(ACT_GH),
                  pl.BlockSpec((1, HG_DV), lambda i: (0, 0)),
                  pl.BlockSpec((None, bb, HG_HEADS, HG_DK, HG_DV),
                               lambda i: (layer, i, 0, 0, 0))] + prev_specs,
        out_specs=[pl.BlockSpec((bb, 1, HG_WIDTH), lambda i: (i, 0, 0)),
                   pl.BlockSpec((None, bb, HG_HEADS, HG_DK, HG_DV),
                                lambda i: (layer, i, 0, 0, 0))],
        out_shape=[jax.ShapeDtypeStruct((m, 1, HG_WIDTH), F32),
                   jax.ShapeDtypeStruct((DEPTH, m, HG_HEADS, HG_DK, HG_DV), F32)],
        input_output_aliases={} if stacked is None else {n_in: 1},
        compiler_params=_cparams(("parallel",)),
        name="hgrn_sample",
    )(act3, fg3, act3, act3, gn, state, *prev_args)


def _post_kernel(x_ref, ya_ref, yb_ref, ga_ref, gb_ref, g2_ref, wpa_ref, wpb_ref, wo_ref,
                 wup_ref, wdn_ref, o_ref):
    tm = x_ref.shape[0]
    rc = min(tm, ROW_CHUNK)
    chunks = [slice(c * rc, (c + 1) * rc) for c in range(tm // rc)]
    dot = functools.partial(jnp.dot, preferred_element_type=F32)

    branches = [(dot(ya_ref[rs, :].astype(BF16), wpa_ref[...]),
                 dot(yb_ref[rs, :].astype(BF16), wpb_ref[...])) for rs in chunks]
    x1 = []
    for rs, (pa, pb) in zip(chunks, branches):
        merged = ga_ref[rs, :].astype(F32) * pa + gb_ref[rs, :].astype(F32) * pb
        x1.append(x_ref[rs, :] + dot(merged.astype(BF16), wo_ref[...]))
    up = [dot((_rms(x) * g2_ref[...]).astype(BF16), wup_ref[...]) for x in x1]
    for rs, x, u in zip(chunks, x1, up):
        act = jnp.square(jnp.maximum(u, 0.0)).astype(BF16)
        o_ref[rs, :] = x + dot(act, wdn_ref[...])


def _post(x, ya, yb, act, g2, wpa, wpb, wo, wup, wdn, tm, layer):
    m = x.shape[0]
    row = lambda w: pl.BlockSpec((tm, w), lambda i: (i, 0))
    resident = lambda a: pl.BlockSpec((None,) + a.shape[1:], lambda i: (layer, 0, 0),
                                      pipeline_mode=pl.Buffered(1))
    return pl.pallas_call(
        _post_kernel,
        grid=(m // tm,),
        in_specs=[row(D_MODEL), row(GROUP_WIDTH), row(HG_WIDTH),
                  pl.BlockSpec((tm, HG_WIDTH), lambda i: (i, ACT_GA)),
                  pl.BlockSpec((tm, HG_WIDTH), lambda i: (i, ACT_GB)),
                  pl.BlockSpec((1, D_MODEL), lambda i: (0, 0)),
                  resident(wpa), resident(wpb), resident(wo), resident(wup), resident(wdn)],
        out_specs=row(D_MODEL),
        out_shape=jax.ShapeDtypeStruct((m, D_MODEL), F32),
        compiler_params=pltpu.CompilerParams(dimension_semantics=("parallel",),
                                             vmem_limit_bytes=POST_VMEM_LIMIT),
        name="post",
    )(x, ya, yb, act, act, g2, wpa, wpb, wo, wup, wdn)


def _rope_tables(pos):
    half = HEAD_DIM // 2
    inv = ROPE_THETA ** (-(jnp.arange(half, dtype=F32) * 2.0 / HEAD_DIM))
    ang = pos.astype(F32)[:, None] * inv[None, :]
    cos, sin = jnp.cos(ang), jnp.sin(ang)
    return jnp.concatenate([cos, cos], axis=-1), jnp.concatenate([-sin, sin], axis=-1)


def kernel(x_prompt, x_sample, cache_k_w128, cache_v_w128, cache_k_w512, cache_v_w512,
           cache_k_w2048, cache_v_w2048, state_hgrn, w_in, norm1, q_norm, k_norm, hg_out_norm,
           hg_lower_bounds, w_pa, w_pb, w_o, norm2, w_up, w_down):
    bsz, seq, _ = x_prompt.shape
    dec_b, dec_t, _ = x_sample.shape
    mp = bsz * seq
    ms = dec_b * dec_t

    lb_all = jnp.cumsum(jax.nn.softmax(hg_lower_bounds.astype(F32), axis=0), axis=0)
    lb_all = lb_all - lb_all[0:1]

    cos_p, sin_p = _rope_tables(jnp.arange(seq))
    cos_s, sin_s = _rope_tables(jnp.broadcast_to(PAST_LEN + jnp.arange(dec_t), (ms,)))

    caches = []
    for (win, dil), (ck, cv) in zip(ATTN_GROUPS, ((cache_k_w128, cache_v_w128),
                                                   (cache_k_w512, cache_v_w512),
                                                   (cache_k_w2048, cache_v_w2048))):
        wc = ck.shape[2]
        shape = (DEPTH, dec_b, wc // dil, dil, ATTN_HEADS, HEAD_DIM)
        caches += [ck.reshape(shape), cv.reshape(shape)]

    w_in_b, wpa, wpb, wo, wup, wdn = (w.astype(BF16) for w in (w_in, w_pa, w_pb, w_o, w_up, w_down))

    yp = x_prompt.reshape(mp, D_MODEL)
    ys = x_sample.reshape(ms, D_MODEL)
    keep_p = tuple(min(win, seq) for win, _ in ATTN_GROUPS)
    p_rows, s_rows = None, None
    p_states, s_states = [], None

    for layer in range(DEPTH):
        g1 = norm1[layer].reshape(1, D_MODEL)
        g2 = norm2[layer].reshape(1, D_MODEL)
        gqk = jnp.concatenate([jnp.tile(q_norm[layer], (1, ATTN_HEADS)).reshape(1, QKV_WIDTH),
                               jnp.tile(k_norm[layer], (1, ATTN_HEADS)).reshape(1, QKV_WIDTH)],
                              axis=-1)
        gn = hg_out_norm[layer].reshape(1, HG_DV)
        lb = lb_all[layer].reshape(1, HG_WIDTH)

        act, fg = _in_proj_gate(yp, g1, w_in_b, lb, tm=1024, layer=layer)
        qkv, p_rows = _in_proj_qkv(yp, g1, w_in_b, gqk, cos_p, sin_p, tm=1024, layer=layer,
                                   n_seq=bsz, keep=keep_p, stacked=p_rows)
        ya = _attn_prompt(qkv.reshape(bsz, seq, 3 * QKV_WIDTH))
        yb, st = _hgrn_prompt(act.reshape(bsz, seq, ACT_WIDTH), fg.reshape(bsz, seq, HG_WIDTH), gn)
        yp = _post(yp, ya.reshape(mp, GROUP_WIDTH), yb.reshape(mp, HG_WIDTH), act, g2,
                   wpa, wpb, wo, wup, wdn, tm=512, layer=layer)
        p_states.append(st)

        act_s, fg_s = _in_proj_gate(ys, g1, w_in_b, lb, tm=ms, layer=layer)
        qkv_s, s_rows = _in_proj_qkv(ys, g1, w_in_b, gqk, cos_s, sin_s, tm=ms, layer=layer,
                                     n_seq=1, keep=(ms,) * N_GROUPS, stacked=s_rows)
        ya_s = _attn_sample(qkv_s.reshape(ms, 1, 3 * QKV_WIDTH), caches, layer, bb=8)
        yb_s, s_states = _hgrn_sample(act_s.reshape(ms, 1, ACT_WIDTH), fg_s.reshape(ms, 1, HG_WIDTH),
                                      gn, state_hgrn, layer, bb=8, stacked=s_states)
        ys = _post(ys, ya_s.reshape(ms, GROUP_WIDTH), yb_s.reshape(ms, HG_WIDTH), act_s, g2,
                   wpa, wpb, wo, wup, wdn, tm=ms, layer=layer)

    def interleave_kv(rows, n, keep):
        return [rows[c * N_GROUPS + g].reshape(DEPTH, n, keep[g], ATTN_HEADS, HEAD_DIM)
                for g in range(N_GROUPS) for c in range(2)]

    outs = [yp.reshape(bsz, seq, D_MODEL), ys.reshape(dec_b, dec_t, D_MODEL)]
    outs += interleave_kv(p_rows, bsz, keep_p)
    outs.append(jnp.stack(p_states, axis=0))
    outs += interleave_kv(s_rows, dec_b, (dec_t,) * N_GROUPS)
    outs.append(s_states)
    return tuple(outs)
```

```python
import functools

import jax
import jax.numpy as jnp
from jax import lax
from jax.experimental import pallas as pl
from jax.experimental.pallas import tpu as pltpu

F32 = jnp.float32
BF16 = jnp.bfloat16

D_MODEL = 1024
DEPTH = 4
PAST_LEN = 2048
ATTN_GROUPS = ((128, 1), (512, 4), (2048, 16))
N_GROUPS = 3
ATTN_HEADS = 4
HEAD_DIM = 128
GROUP_WIDTH = ATTN_HEADS * HEAD_DIM
QKV_WIDTH = N_GROUPS * GROUP_WIDTH
ATTN_NK = 128
ATTN_SCALE = HEAD_DIM ** -0.5
HG_HEADS = 8
HG_DK = 128
HG_DV = 128
HG_WIDTH = 1024
HG_CHUNK = 16
HG_BLOCK = 128
HG_HEAD_UNROLL = 8
D_FF = 4 * D_MODEL
ROPE_THETA = 10000.0
NORM_EPS = 1e-6
IN_WIDTH = 3 * QKV_WIDTH + 6 * HG_WIDTH
GATE_WIDTH = 6 * HG_WIDTH
NEG_BIG = -1e30

SEG_QH, SEG_FH, SEG_IH, SEG_GH, SEG_GA, SEG_GB = range(6)
ACT_Q, ACT_V, ACT_GH, ACT_GA, ACT_GB = range(5)
ACT_WIDTH = 5 * HG_WIDTH
COL_Q, COL_K, COL_V = 0, QKV_WIDTH, 2 * QKV_WIDTH

GATE_TILE = 1536
ROW_CHUNK = 256
ATTN_UNROLL = 8
SAMPLE_UNROLL = 2

VMEM_LIMIT = 48 * 1024 * 1024
POST_VMEM_LIMIT = 56 * 1024 * 1024


def _cparams(sem):
    return pltpu.CompilerParams(dimension_semantics=sem, vmem_limit_bytes=VMEM_LIMIT)


def _rms(x):
    return x * lax.rsqrt(jnp.mean(x * x, axis=-1, keepdims=True) + NORM_EPS)


def _dot_nt(a, b):
    return lax.dot_general(a, b, (((1,), (1,)), ((), ())), preferred_element_type=F32)


def _in_proj_gate_kernel(x_ref, g1_ref, w_ref, lb_ref, act_ref, fg_ref, h_ref):
    j = pl.program_id(1)

    @pl.when(j == 0)
    def _():
        h_ref[...] = (_rms(x_ref[...]) * g1_ref[...]).astype(BF16)

    tm = x_ref.shape[0]
    rc = min(tm, ROW_CHUNK)
    piece = GROUP_WIDTH
    act_slot = {SEG_QH: ACT_Q, SEG_IH: ACT_V, SEG_GH: ACT_GH, SEG_GA: ACT_GA, SEG_GB: ACT_GB}

    def tile(jj):
        for c in range(tm // rc):
            rs = slice(c * rc, (c + 1) * rc)
            res = jnp.dot(h_ref[rs, :], w_ref[...], preferred_element_type=F32)
            for p in range(GATE_TILE // piece):
                col = jj * GATE_TILE + p * piece
                seg, off = col // HG_WIDTH, col % HG_WIDTH
                y = res[:, p * piece:(p + 1) * piece]
                if seg == SEG_FH:
                    lb = lb_ref[:, off:off + piece]
                    fg_ref[rs, off:off + piece] = lb + (1.0 - lb) * jax.nn.sigmoid(y)
                    continue
                if seg == SEG_QH:
                    y = jax.nn.silu(y)
                elif seg != SEG_IH:
                    y = jax.nn.sigmoid(y)
                dst = act_slot[seg] * HG_WIDTH + off
                act_ref[rs, dst:dst + piece] = y.astype(BF16)

    for jj in range(GATE_WIDTH // GATE_TILE):
        pl.when(j == jj)(functools.partial(tile, jj))


def _in_proj_gate(x, g1, w, lb, tm, layer):
    m = x.shape[0]
    first = 3 * QKV_WIDTH // GATE_TILE
    return pl.pallas_call(
        _in_proj_gate_kernel,
        grid=(m // tm, GATE_WIDTH // GATE_TILE),
        in_specs=[
            pl.BlockSpec((tm, D_MODEL), lambda i, j: (i, 0)),
            pl.BlockSpec((1, D_MODEL), lambda i, j: (0, 0)),
            pl.BlockSpec((None, D_MODEL, GATE_TILE), lambda i, j: (layer, 0, first + j)),
            pl.BlockSpec((1, HG_WIDTH), lambda i, j: (0, 0)),
        ],
        out_specs=[pl.BlockSpec((tm, ACT_WIDTH), lambda i, j: (i, 0)),
                   pl.BlockSpec((tm, HG_WIDTH), lambda i, j: (i, 0))],
        out_shape=[jax.ShapeDtypeStruct((m, ACT_WIDTH), BF16),
                   jax.ShapeDtypeStruct((m, HG_WIDTH), F32)],
        scratch_shapes=[pltpu.VMEM((tm, D_MODEL), BF16)],
        compiler_params=_cparams(("parallel", "arbitrary")),
        name="in_proj_gate",
    )(x, g1, w, lb)


def _in_proj_qkv_kernel(keep_in_tile, x_ref, g1_ref, w_ref, gqk_ref, cos_ref, sin_ref, *rest):
    o_ref, h_ref = rest[-8], rest[-1]
    k_rows, v_rows = rest[-7:-4], rest[-4:-1]
    j = pl.program_id(1)

    @pl.when(j == 0)
    def _():
        h_ref[...] = (_rms(x_ref[...]) * g1_ref[...]).astype(BF16)

    tm = x_ref.shape[0]
    rc = min(tm, ROW_CHUNK)

    def tile(rope, row_refs):
        for c in range(tm // rc):
            rs = slice(c * rc, (c + 1) * rc)
            for g in range(N_GROUPS):
                res = jnp.dot(h_ref[rs, :], w_ref[:, g * GROUP_WIDTH:(g + 1) * GROUP_WIDTH],
                              preferred_element_type=F32)
                first_kept = tm - keep_in_tile[g]
                lo = max(c * rc, first_kept)
                for h in range(ATTN_HEADS):
                    cs = slice(g * GROUP_WIDTH + h * HEAD_DIM, g * GROUP_WIDTH + (h + 1) * HEAD_DIM)
                    y = res[:, h * HEAD_DIM:(h + 1) * HEAD_DIM]
                    if rope:
                        y = _rms(y) * gqk_ref[:, cs]
                        y = y * cos_ref[rs, :] + pltpu.roll(y, HEAD_DIM // 2, 1) * sin_ref[rs, :]
                    o_ref[rs, cs] = y
                    if row_refs is not None and lo < (c + 1) * rc:
                        n = (c + 1) * rc - lo
                        dst = (lo - first_kept) * ATTN_HEADS + h
                        row_refs[g][pl.ds(dst, n, stride=ATTN_HEADS), :] = y[lo - c * rc:, :]

    pl.when(j == 0)(lambda: tile(True, None))
    pl.when(j == 1)(lambda: tile(True, k_rows))
    pl.when(j == 2)(lambda: tile(False, v_rows))


def _in_proj_qkv(x, g1, w, gqk, cos, sin, tm, layer, n_seq, keep, stacked):
    m = x.shape[0]
    n_pos_blocks = cos.shape[0] // tm
    tiles_per_seq = m // n_seq // tm
    keep_in_tile = tuple(min(k, tm) for k in keep)
    row_specs, row_shapes = [], []
    for k, kt in zip(keep, keep_in_tile):
        first_tile = tiles_per_seq - k // kt

        def index(i, j, first_tile=first_tile):
            return (layer, i // tiles_per_seq, jnp.maximum(i % tiles_per_seq - first_tile, 0), 0)

        row_specs.append(pl.BlockSpec((None, None, kt * ATTN_HEADS, HEAD_DIM), index))
        row_shapes.append(jax.ShapeDtypeStruct((DEPTH, n_seq, k * ATTN_HEADS, HEAD_DIM), F32))
    n_in = 6
    prev_specs = [] if stacked is None else [pl.BlockSpec(memory_space=pl.ANY)] * 6
    prev_args = [] if stacked is None else list(stacked)
    aliases = {} if stacked is None else {n_in + r: 1 + r for r in range(6)}
    outs = pl.pallas_call(
        functools.partial(_in_proj_qkv_kernel, keep_in_tile),
        grid=(m // tm, 3),
        in_specs=[
            pl.BlockSpec((tm, D_MODEL), lambda i, j: (i, 0)),
            pl.BlockSpec((1, D_MODEL), lambda i, j: (0, 0)),
            pl.BlockSpec((None, D_MODEL, QKV_WIDTH), lambda i, j: (layer, 0, j)),
            pl.BlockSpec((1, QKV_WIDTH), lambda i, j: (0, jnp.minimum(j, 1))),
            pl.BlockSpec((tm, HEAD_DIM), lambda i, j: (i % n_pos_blocks, 0)),
            pl.BlockSpec((tm, HEAD_DIM), lambda i, j: (i % n_pos_blocks, 0)),
        ] + prev_specs,
        out_specs=[pl.BlockSpec((tm, QKV_WIDTH), lambda i, j: (i, j))] + row_specs * 2,
        out_shape=[jax.ShapeDtypeStruct((m, 3 * QKV_WIDTH), F32)] + row_shapes * 2,
        input_output_aliases=aliases,
        scratch_shapes=[pltpu.VMEM((tm, D_MODEL), BF16)],
        compiler_params=_cparams(("arbitrary", "arbitrary")),
        name="in_proj_qkv",
    )(x, g1, w, gqk, cos, sin, *prev_args)
    return outs[0], outs[1:]


def _attn_prompt_kernel(q0, q1, q2, k0, k1, k2, v0, v1, v2, o_ref, m_sc, l_sc, a_sc):
    seq = o_ref.shape[0]
    nk = ATTN_NK
    row = lax.broadcasted_iota(jnp.int32, (nk, nk), 0)
    col = lax.broadcasted_iota(jnp.int32, (nk, nk), 1)
    mask_cur = col <= row
    mask_prev = col >= row

    def rows(start, stride):
        if stride == 1:
            return pl.ds(start, nk)
        return pl.ds(start, nk, stride=stride)

    def load(qkv, dil, starts):
        q_ref, k_ref, v_ref = qkv
        loaded = []
        for start, prev_start in starts:
            sl = rows(start, dil)
            item = [q_ref[sl, :], k_ref[sl, :], v_ref[sl, :]]
            if prev_start is not None:
                slp = rows(prev_start, dil)
                item += [k_ref[slp, :], v_ref[slp, :]]
            loaded.append(item)
        return loaded

    def attend(loaded):
        scored = []
        for item in loaded:
            q = item[0].astype(BF16)
            s = [jnp.where(mask_cur, _dot_nt(q, item[1].astype(BF16)) * ATTN_SCALE, NEG_BIG)]
            if len(item) > 3:
                s.append(jnp.where(mask_prev, _dot_nt(q, item[3].astype(BF16)) * ATTN_SCALE,
                                   NEG_BIG))
            scored.append(s)
        probs = []
        for s in scored:
            m = jnp.max(s[0] if len(s) == 1 else jnp.maximum(s[0], s[1]), axis=-1, keepdims=True)
            p = [jnp.exp(x - m) for x in s]
            l = jnp.sum(p[0] if len(s) == 1 else p[0] + p[1], axis=-1, keepdims=True)
            probs.append((m, l, [x.astype(BF16) for x in p]))
        results = []
        for item, (m, l, p) in zip(loaded, probs):
            acc = jnp.dot(p[0], item[2].astype(BF16), preferred_element_type=F32)
            if len(p) > 1:
                acc = acc + jnp.dot(p[1], item[4].astype(BF16), preferred_element_type=F32)
            results.append((m, l, acc))
        return results

    def keep(g, dil, starts, results):
        for (start, _), (m, l, acc) in zip(starts, results):
            sl = rows(start, dil)
            m_sc[g, sl, :] = jnp.broadcast_to(m, (nk, HEAD_DIM))
            l_sc[g, sl, :] = jnp.broadcast_to(l, (nk, HEAD_DIM))
            a_sc[g, sl, :] = acc

    def blocks(g, qkv, dil, starts):
        keep(g, dil, starts, attend(load(qkv, dil, starts)))

    refs = ((q0, k0, v0), (q1, k1, v1), (q2, k2, v2))
    unr = ATTN_UNROLL
    for g, (_, dil) in enumerate(ATTN_GROUPS):
        qkv = refs[g]
        nb = seq // dil // nk
        stride = nk * dil
        if nb == 1:
            def body(it, carry, g=g, qkv=qkv, dil=dil):
                blocks(g, qkv, dil, [(it * unr + u, None) for u in range(unr)])
                return carry
            lax.fori_loop(0, dil // unr, body, 0)
        elif dil == 1:
            blocks(g, qkv, dil, [(n * nk, None if n == 0 else (n - 1) * nk) for n in range(unr)])

            def body(it, carry, g=g, qkv=qkv, dil=dil):
                starts = [pl.multiple_of((it * unr + u) * nk, nk) for u in range(unr)]
                blocks(g, qkv, dil, [(s, s - nk) for s in starts])
                return carry
            lax.fori_loop(1, nb // unr, body, 0)
        else:
            blocks(g, qkv, dil, [(r, None) for r in range(dil)])

            def body(n, carry, g=g, qkv=qkv, dil=dil, stride=stride):
                blocks(g, qkv, dil, [(r + n * stride, r + (n - 1) * stride) for r in range(dil)])
                return carry
            lax.fori_loop(1, nb, body, 0)

    m_all = jnp.maximum(jnp.maximum(m_sc[0], m_sc[1]), m_sc[2])
    num = jnp.zeros((seq, HEAD_DIM), F32)
    den = jnp.zeros((seq, HEAD_DIM), F32)
    for g in range(N_GROUPS):
        w = jnp.exp(m_sc[g] - m_all)
        num = num + w * a_sc[g]
        den = den + w * l_sc[g]
    o_ref[...] = (num / den).astype(o_ref.dtype)


def _attn_prompt(qkv3):
    b, seq, _ = qkv3.shape

    def spec(col0, g):
        base = (col0 + g * GROUP_WIDTH) // HEAD_DIM
        return pl.BlockSpec((None, seq, HEAD_DIM), lambda bi, h, base=base: (bi, 0, base + h))

    in_specs = ([spec(COL_Q, g) for g in range(N_GROUPS)]
                + [spec(COL_K, g) for g in range(N_GROUPS)]
                + [spec(COL_V, g) for g in range(N_GROUPS)])
    return pl.pallas_call(
        _attn_prompt_kernel,
        grid=(b, ATTN_HEADS),
        in_specs=in_specs,
        out_specs=pl.BlockSpec((None, seq, HEAD_DIM), lambda bi, h: (bi, 0, h)),
        out_shape=jax.ShapeDtypeStruct((b, seq, GROUP_WIDTH), BF16),
        scratch_shapes=[pltpu.VMEM((N_GROUPS, seq, HEAD_DIM), F32)] * 3,
        compiler_params=_cparams(("parallel", "parallel")),
        name="attn_prompt",
    )(*([qkv3] * 9))


def _attn_sample_kernel(qkv_ref, kc0, vc0, kc1, vc1, kc2, vc2, o_ref):
    bb = o_ref.shape[0]
    caches = ((kc0, vc0), (kc1, vc1), (kc2, vc2))
    n_rows = ATTN_NK * ATTN_HEADS
    r8 = lax.broadcasted_iota(jnp.int32, (8, n_rows), 0)
    c8 = lax.broadcasted_iota(jnp.int32, (8, n_rows), 1)
    own = (r8 & (ATTN_HEADS - 1)) == (c8 & (ATTN_HEADS - 1))

    def window(cache_ref, b):
        return cache_ref[b].reshape(n_rows, HEAD_DIM).astype(BF16)

    def heads_on_rows(row, col0):
        hs = [row[:, col0 + h * HEAD_DIM:col0 + (h + 1) * HEAD_DIM] for h in range(ATTN_HEADS)]
        return jnp.concatenate(hs + hs, axis=0)

    def body(it, carry):
        items = []
        for u in range(SAMPLE_UNROLL):
            b = it * SAMPLE_UNROLL + u
            row = qkv_ref[b]
            for g in range(N_GROUPS):
                lo = g * GROUP_WIDTH
                q = heads_on_rows(row, COL_Q + lo)
                kn = heads_on_rows(row, COL_K + lo)
                vn = heads_on_rows(row, COL_V + lo)
                s = _dot_nt(q.astype(BF16), window(caches[g][0], b)) * ATTN_SCALE
                s = jnp.where(own, s, NEG_BIG)
                sn = jnp.sum(q * kn, axis=-1, keepdims=True) * ATTN_SCALE
                items.append((b, g, s, sn, vn))
        probs = []
        for b, g, s, sn, vn in items:
            m = jnp.maximum(jnp.max(s, axis=-1, keepdims=True), sn)
            p = jnp.exp(s - m)
            pn = jnp.exp(sn - m)
            probs.append((m, jnp.sum(p, axis=-1, keepdims=True) + pn, p.astype(BF16), pn))
        accs = [jnp.dot(p, window(caches[g][1], b), preferred_element_type=F32) + pn * vn
                for (b, g, _, _, vn), (_, _, p, pn) in zip(items, probs)]
        for u in range(SAMPLE_UNROLL):
            sl = slice(u * N_GROUPS, (u + 1) * N_GROUPS)
            ms = [pr[0] for pr in probs[sl]]
            ls = [pr[1] for pr in probs[sl]]
            m_all = jnp.maximum(jnp.maximum(ms[0], ms[1]), ms[2])
            num = jnp.zeros((8, HEAD_DIM), F32)
            den = jnp.zeros((8, 1), F32)
            for g in range(N_GROUPS):
                w = jnp.exp(ms[g] - m_all)
                num = num + w * accs[sl][g]
                den = den + w * ls[g]
            out = num / den
            o_ref[it * SAMPLE_UNROLL + u] = jnp.concatenate(
                [out[h:h + 1] for h in range(ATTN_HEADS)], axis=1)
        return carry

    lax.fori_loop(0, bb // SAMPLE_UNROLL, body, 0)


def _attn_sample(qkv3, caches, layer, bb):
    m = qkv3.shape[0]
    cache_specs = [pl.BlockSpec((None, bb, ATTN_NK, None, ATTN_HEADS, HEAD_DIM),
                                lambda i: (layer, i, 0, 0, 0, 0))
                   for _ in range(2 * N_GROUPS)]
    return pl.pallas_call(
        _attn_sample_kernel,
        grid=(m // bb,),
        in_specs=[pl.BlockSpec((bb, 1, 3 * QKV_WIDTH), lambda i: (i, 0, 0))] + cache_specs,
        out_specs=pl.BlockSpec((bb, 1, GROUP_WIDTH), lambda i: (i, 0, 0)),
        out_shape=jax.ShapeDtypeStruct((m, 1, GROUP_WIDTH), F32),
        compiler_params=_cparams(("parallel",)),
        name="attn_sample",
    )(qkv3, *caches)


def _cumsum8(y, rid):
    for s in (1, 2, 4):
        y = y + jnp.where(rid >= s, pltpu.roll(y, s, 0), 0.0)
    return y


def _hgrn_prompt_kernel(q_ref, fg_ref, v_ref, gate_ref, gn_ref, o_ref, sfin_ref, st_sc):
    t = pl.program_id(1)
    blk = HG_BLOCK
    c = HG_CHUNK
    nch = blk // c

    @pl.when(t == 0)
    def _():
        st_sc[...] = jnp.zeros_like(st_sc)

    rowi = lax.broadcasted_iota(jnp.int32, (blk, blk), 0)
    coli = lax.broadcasted_iota(jnp.int32, (blk, blk), 1)
    causal = coli <= rowi
    rid8 = lax.broadcasted_iota(jnp.int32, (8, HG_DK), 0)

    def head_scores(q, fg):
        q = q.astype(F32)
        lf = jnp.log(fg)
        kh = 1.0 - fg

        slabs = []
        for k in range(blk // 8):
            s8 = _cumsum8(lf[8 * k:8 * k + 8], rid8)
            if k % 2 == 1:
                s8 = s8 + slabs[k - 1][7:8, :]
            slabs.append(s8)
        g_pre = jnp.zeros((1, HG_DK), F32)
        g_starts, acum_slabs = [], []
        for i in range(nch):
            g_starts.append(g_pre)
            acum_slabs += [slabs[2 * i] + g_pre, slabs[2 * i + 1] + g_pre]
            g_pre = g_pre + slabs[2 * i + 1][7:8, :]
        g_end = g_pre
        a = jnp.concatenate(slabs, axis=0)
        acum = jnp.concatenate(acum_slabs, axis=0)

        qt = (q * jnp.exp(a)).astype(BF16)
        qhat = (q * jnp.exp(acum)).astype(BF16)
        khat = (kh * jnp.exp(g_end - acum)).astype(BF16)

        prow = []
        for i in range(nch):
            n = c * (i + 1)
            kt = (kh[:n] * jnp.exp(g_starts[i] - acum[:n])).astype(BF16)
            if n < blk:
                kt = jnp.concatenate([kt, jnp.zeros((blk - n, HG_DK), BF16)], axis=0)
            prow.append(_dot_nt(qt[c * i:c * (i + 1)], kt))
        return prow, qhat, khat, jnp.exp(g_end)

    def head_output(scores, v, gate, st0):
        prow, qhat, khat, dec = scores
        p = jnp.where(causal, jnp.concatenate(prow, axis=0), 0.0).astype(BF16)
        vt = v.astype(F32).T.astype(BF16)
        o = _dot_nt(jnp.concatenate([p, qhat], axis=1),
                    jnp.concatenate([vt, st0.astype(BF16)], axis=1))
        o = _rms(o) * gn_ref[...] * gate.astype(F32)
        return o, st0 * dec + jnp.dot(vt, khat, preferred_element_type=F32)

    def head_group(i, carry):
        heads = [i * HG_HEAD_UNROLL + u for u in range(HG_HEAD_UNROLL)]
        cols = [pl.ds(pl.multiple_of(h * HG_DK, HG_DK), HG_DK) for h in heads]
        loaded = [(q_ref[:, cs], fg_ref[:, cs], v_ref[:, cs], gate_ref[:, cs], st_sc[h])
                  for h, cs in zip(heads, cols)]
        scores = [head_scores(*args[:2]) for args in loaded]
        results = [head_output(sc, *args[2:]) for sc, args in zip(scores, loaded)]
        for h, cs, (o, st_new) in zip(heads, cols, results):
            o_ref[:, cs] = o.astype(o_ref.dtype)
            st_sc[h] = st_new
        return carry

    lax.fori_loop(0, HG_HEADS // HG_HEAD_UNROLL, head_group, 0)

    @pl.when(t == pl.num_programs(1) - 1)
    def _():
        for h in range(HG_HEADS):
            sfin_ref[h] = st_sc[h].T


def _hgrn_prompt(act3, fg3, gn):
    b, seq, _ = act3.shape

    def seg(blk):
        return pl.BlockSpec((None, HG_BLOCK, HG_WIDTH), lambda bi, t, blk=blk: (bi, t, blk))

    return pl.pallas_call(
        _hgrn_prompt_kernel,
        grid=(b, seq // HG_BLOCK),
        in_specs=[seg(ACT_Q), seg(0), seg(ACT_V), seg(ACT_GH),
                  pl.BlockSpec((1, HG_DV), lambda bi, t: (0, 0))],
        out_specs=[pl.BlockSpec((None, HG_BLOCK, HG_WIDTH), lambda bi, t: (bi, t, 0)),
                   pl.BlockSpec((None, HG_HEADS, HG_DK, HG_DV), lambda bi, t: (bi, 0, 0, 0))],
        out_shape=[jax.ShapeDtypeStruct((b, seq, HG_WIDTH), BF16),
                   jax.ShapeDtypeStruct((b, HG_HEADS, HG_DK, HG_DV), F32)],
        scratch_shapes=[pltpu.VMEM((HG_HEADS, HG_DV, HG_DK), F32)],
        compiler_params=_cparams(("parallel", "arbitrary")),
        name="hgrn_prompt",
    )(act3, fg3, act3, act3, gn)


def _hgrn_sample_kernel(q_ref, fg_ref, v_ref, gate_ref, gn_ref, s_ref, *rest):
    o_ref, sout_ref = rest[-2:]
    bb = q_ref.shape[0]
    ri = lax.broadcasted_iota(jnp.int32, (HG_DK, HG_DK), 0)
    ci = lax.broadcasted_iota(jnp.int32, (HG_DK, HG_DK), 1)
    diag = ri == ci

    def column(rowvec):
        full = jnp.broadcast_to(rowvec, (HG_DK, HG_DK))
        return jnp.sum(jnp.where(diag, full, 0.0), axis=1, keepdims=True)

    def body(b, carry):
        q = q_ref[b].astype(F32)
        fg = fg_ref[b]
        v = v_ref[b].astype(F32)
        gate = gate_ref[b].astype(F32)
        a = jnp.log(fg)
        kh = 1.0 - fg
        qt = q * jnp.exp(a)
        kt = kh * jnp.exp(-a)
        dec = jnp.exp(a)
        for h in range(HG_HEADS):
            hs = slice(h * HG_DK, (h + 1) * HG_DK)
            s0 = s_ref[b, h]
            att = jnp.sum(qt[:, hs] * kt[:, hs], axis=-1, keepdims=True)
            q8 = jnp.broadcast_to(qt[:, hs], (8, HG_DK)).astype(BF16)
            o = att * v[:, hs] + jnp.dot(q8, s0.astype(BF16), preferred_element_type=F32)[0:1]
            sout_ref[b, h] = column(dec[:, hs]) * s0 + column(kh[:, hs]) * v[:, hs]
            o_ref[b, :, hs] = _rms(o) * gn_ref[...] * gate[:, hs]
        return carry

    lax.fori_loop(0, bb, body, 0)


def _hgrn_sample(act3, fg3, gn, state, layer, bb, stacked):
    m = act3.shape[0]
    prev_specs = [] if stacked is None else [pl.BlockSpec(memory_space=pl.ANY)]
    prev_args = [] if stacked is None else [stacked]
    n_in = 6

    def seg(blk):
        return pl.BlockSpec((bb, 1, HG_WIDTH), lambda i, blk=blk: (i, 0, blk))

    return pl.pallas_call(
        _hgrn_sample_kernel,
        grid=(m // bb,),
        in_specs=[seg(ACT_Q), seg(0), seg(ACT_V), seg(ACT_GH),
                  pl.BlockSpec((1, HG_DV), lambda i: (0, 0)),
                  pl.BlockSpec((None, bb, HG_HEADS, HG_DK, HG_DV),
                               lambda i: (layer, i, 0, 0, 0))] + prev_specs,
        out_specs=[pl.BlockSpec((bb, 1, HG_WIDTH), lambda i: (i, 0, 0)),
                   pl.BlockSpec((None, bb, HG_HEADS, HG_DK, HG_DV),
                                lambda i: (layer, i, 0, 0, 0))],
        out_shape=[jax.ShapeDtypeStruct((m, 1, HG_WIDTH), F32),
                   jax.ShapeDtypeStruct((DEPTH, m, HG_HEADS, HG_DK, HG_DV), F32)],
        input_output_aliases={} if stacked is None else {n_in: 1},
        compiler_params=_cparams(("parallel",)),
        name="hgrn_sample",
    )(act3, fg3, act3, act3, gn, state, *prev_args)


def _post_kernel(x_ref, ya_ref, yb_ref, ga_ref, gb_ref, g2_ref, wpa_ref, wpb_ref, wo_ref,
                 wup_ref, wdn_ref, o_ref):
    tm = x_ref.shape[0]
    rc = min(tm, ROW_CHUNK)
    chunks = [slice(c * rc, (c + 1) * rc) for c in range(tm // rc)]
    dot = functools.partial(jnp.dot, preferred_element_type=F32)

    branches = [(dot(ya_ref[rs, :].astype(BF16), wpa_ref[...]),
                 dot(yb_ref[rs, :].astype(BF16), wpb_ref[...])) for rs in chunks]
    x1 = []
    for rs, (pa, pb) in zip(chunks, branches):
        merged = ga_ref[rs, :].astype(F32) * pa + gb_ref[rs, :].astype(F32) * pb
        x1.append(x_ref[rs, :] + dot(merged.astype(BF16), wo_ref[...]))
    up = [dot((_rms(x) * g2_ref[...]).astype(BF16), wup_ref[...]) for x in x1]
    for rs, x, u in zip(chunks, x1, up):
        act = jnp.square(jnp.maximum(u, 0.0)).astype(BF16)
        o_ref[rs, :] = x + dot(act, wdn_ref[...])


def _post(x, ya, yb, act, g2, wpa, wpb, wo, wup, wdn, tm, layer):
    m = x.shape[0]
    row = lambda w: pl.BlockSpec((tm, w), lambda i: (i, 0))
    resident = lambda a: pl.BlockSpec((None,) + a.shape[1:], lambda i: (layer, 0, 0),
                                      pipeline_mode=pl.Buffered(1))
    return pl.pallas_call(
        _post_kernel,
        grid=(m // tm,),
        in_specs=[row(D_MODEL), row(GROUP_WIDTH), row(HG_WIDTH),
                  pl.BlockSpec((tm, HG_WIDTH), lambda i: (i, ACT_GA)),
                  pl.BlockSpec((tm, HG_WIDTH), lambda i: (i, ACT_GB)),
                  pl.BlockSpec((1, D_MODEL), lambda i: (0, 0)),
                  resident(wpa), resident(wpb), resident(wo), resident(wup), resident(wdn)],
        out_specs=row(D_MODEL),
        out_shape=jax.ShapeDtypeStruct((m, D_MODEL), F32),
        compiler_params=pltpu.CompilerParams(dimension_semantics=("parallel",),
                                             vmem_limit_bytes=POST_VMEM_LIMIT),
        name="post",
    )(x, ya, yb, act, act, g2, wpa, wpb, wo, wup, wdn)


def _rope_tables(pos):
    half = HEAD_DIM // 2
    inv = ROPE_THETA ** (-(jnp.arange(half, dtype=F32) * 2.0 / HEAD_DIM))
    ang = pos.astype(F32)[:, None] * inv[None, :]
    cos, sin = jnp.cos(ang), jnp.sin(ang)
    return jnp.concatenate([cos, cos], axis=-1), jnp.concatenate([-sin, sin], axis=-1)


def kernel(x_prompt, x_sample, cache_k_w128, cache_v_w128, cache_k_w512, cache_v_w512,
           cache_k_w2048, cache_v_w2048, state_hgrn, w_in, norm1, q_norm, k_norm, hg_out_norm,
           hg_lower_bounds, w_pa, w_pb, w_o, norm2, w_up, w_down):
    bsz, seq, _ = x_prompt.shape
    dec_b, dec_t, _ = x_sample.shape
    mp = bsz * seq
    ms = dec_b * dec_t

    lb_all = jnp.cumsum(jax.nn.softmax(hg_lower_bounds.astype(F32), axis=0), axis=0)
    lb_all = lb_all - lb_all[0:1]

    cos_p, sin_p = _rope_tables(jnp.arange(seq))
    cos_s, sin_s = _rope_tables(jnp.broadcast_to(PAST_LEN + jnp.arange(dec_t), (ms,)))

    caches = []
    for (win, dil), (ck, cv) in zip(ATTN_GROUPS, ((cache_k_w128, cache_v_w128),
                                                   (cache_k_w512, cache_v_w512),
                                                   (cache_k_w2048, cache_v_w2048))):
        wc = ck.shape[2]
        shape = (DEPTH, dec_b, wc // dil, dil, ATTN_HEADS, HEAD_DIM)
        caches += [ck.reshape(shape), cv.reshape(shape)]

    w_in_b, wpa, wpb, wo, wup, wdn = (w.astype(BF16) for w in (w_in, w_pa, w_pb, w_o, w_up, w_down))

    yp = x_prompt.reshape(mp, D_MODEL)
    ys = x_sample.reshape(ms, D_MODEL)
    keep_p = tuple(min(win, seq) for win, _ in ATTN_GROUPS)
    p_rows, s_rows = None, None
    p_states, s_states = [], None

    for layer in range(DEPTH):
        g1 = norm1[layer].reshape(1, D_MODEL)
        g2 = norm2[layer].reshape(1, D_MODEL)
        gqk = jnp.concatenate([jnp.tile(q_norm[layer], (1, ATTN_HEADS)).reshape(1, QKV_WIDTH),
                               jnp.tile(k_norm[layer], (1, ATTN_HEADS)).reshape(1, QKV_WIDTH)],
                              axis=-1)
        gn = hg_out_norm[layer].reshape(1, HG_DV)
        lb = lb_all[layer].reshape(1, HG_WIDTH)

        act, fg = _in_proj_gate(yp, g1, w_in_b, lb, tm=1024, layer=layer)
        qkv, p_rows = _in_proj_qkv(yp, g1, w_in_b, gqk, cos_p, sin_p, tm=1024, layer=layer,
                                   n_seq=bsz, keep=keep_p, stacked=p_rows)
        ya = _attn_prompt(qkv.reshape(bsz, seq, 3 * QKV_WIDTH))
        yb, st = _hgrn_prompt(act.reshape(bsz, seq, ACT_WIDTH), fg.reshape(bsz, seq, HG_WIDTH), gn)
        yp = _post(yp, ya.reshape(mp, GROUP_WIDTH), yb.reshape(mp, HG_WIDTH), act, g2,
                   wpa, wpb, wo, wup, wdn, tm=512, layer=layer)
        p_states.append(st)

        act_s, fg_s = _in_proj_gate(ys, g1, w_in_b, lb, tm=ms, layer=layer)
        qkv_s, s_rows = _in_proj_qkv(ys, g1, w_in_b, gqk, cos_s, sin_s, tm=ms, layer=layer,
                                     n_seq=1, keep=(ms,) * N_GROUPS, stacked=s_rows)
        ya_s = _attn_sample(qkv_s.reshape(ms, 1, 3 * QKV_WIDTH), caches, layer, bb=8)
        yb_s, s_states = _hgrn_sample(act_s.reshape(ms, 1, ACT_WIDTH), fg_s.reshape(ms, 1, HG_WIDTH),
                                      gn, state_hgrn, layer, bb=8, stacked=s_states)
        ys = _post(ys, ya_s.reshape(ms, GROUP_WIDTH), yb_s.reshape(ms, HG_WIDTH), act_s, g2,
                   wpa, wpb, wo, wup, wdn, tm=ms, layer=layer)

    def interleave_kv(rows, n, keep):
        return [rows[c * N_GROUPS + g].reshape(DEPTH, n, keep[g], ATTN_HEADS, HEAD_DIM)
                for g in range(N_GROUPS) for c in range(2)]

    outs = [yp.reshape(bsz, seq, D_MODEL), ys.reshape(dec_b, dec_t, D_MODEL)]
    outs += interleave_kv(p_rows, bsz, keep_p)
    outs.append(jnp.stack(p_states, axis=0))
    outs += interleave_kv(s_rows, dec_b, (dec_t,) * N_GROUPS)
    outs.append(s_states)
    return tuple(outs)
```
